```python
import jax, jax.numpy as jnp
from jax import lax
import numpy as np

D_MODEL = 2048
BATCH = 2
SEQ = 8192
DEPTH = 1
DEC_BATCH = 32
DEC_SEQ = 8
PAST_LEN = 16384
PAGE_SIZE = 128

CONV_CH = D_MODEL // 2
N_HEADS = 8
HEAD_DIM = (D_MODEL - CONV_CH) // N_HEADS
N_KV_HEADS = 2
MIX_WIDTH = CONV_CH + N_HEADS * HEAD_DIM
N_IDX_HEADS = 16
IDX_DIM = 64
TOPK_MAX = 256
CONV_WIDTH = 31
D_FF = -(-8 * D_MODEL // (3 * 256)) * 256
QBLOCK = 128
EPS = 1e-6
ATTN_SCALE = HEAD_DIM ** -0.5
INDEX_SCALE = (N_IDX_HEADS * IDX_DIM) ** -0.5
IN_SIZES = (CONV_CH, CONV_CH, N_HEADS * HEAD_DIM, N_KV_HEADS * HEAD_DIM,
            N_KV_HEADS * HEAD_DIM, N_IDX_HEADS * IDX_DIM, IDX_DIM, N_IDX_HEADS)
N_IN = sum(IN_SIZES)

kernel_name = "hymba_conformer_conv_dsa_sparse_attn_step"


def rms_norm(x, g):
    xf = x.astype(jnp.float32)
    y = xf * lax.rsqrt(jnp.mean(xf * xf, axis=-1, keepdims=True) + EPS)
    return (y * g.astype(jnp.float32)).astype(x.dtype)


def layer_norm(x, g, b):
    xf = x.astype(jnp.float32)
    mu = jnp.mean(xf, axis=-1, keepdims=True)
    xc = xf - mu
    y = xc * lax.rsqrt(jnp.mean(xc * xc, axis=-1, keepdims=True) + EPS)
    return (y * g.astype(jnp.float32) + b.astype(jnp.float32)).astype(x.dtype)


def project_in(x, g_pre, w_in):
    B, T, _ = x.shape
    proj = rms_norm(x, g_pre) @ w_in
    offs = np.cumsum(IN_SIZES)[:-1].tolist()
    u_a, u_g, q, k, v, qi, ki, wi = jnp.split(proj, offs, axis=-1)
    return (u_a, u_g,
            q.reshape(B, T, N_HEADS, HEAD_DIM),
            k.reshape(B, T, N_KV_HEADS, HEAD_DIM),
            v.reshape(B, T, N_KV_HEADS, HEAD_DIM),
            qi.reshape(B, T, N_IDX_HEADS, IDX_DIM),
            ki, wi)


def conformer_conv(u_a, u_g, hist, conv_w, conv_b, ln_g, ln_b):
    u = u_a * jax.nn.sigmoid(u_g)
    full = jnp.concatenate([hist.astype(u.dtype), u], axis=1)
    y = lax.conv_general_dilated(
        full, conv_w[:, None, :].astype(full.dtype), window_strides=(1,),
        padding='VALID', dimension_numbers=('NWC', 'WIO', 'NWC'),
        feature_group_count=CONV_CH)
    y = y + conv_b.astype(y.dtype)
    y = jax.nn.silu(layer_norm(y, ln_g, ln_b))
    return y, full[:, -(CONV_WIDTH - 1):]


def indexer_topk(qi, wi, k_idx, q_pos, k_top):
    L = k_idx.shape[1]
    logits = jnp.einsum('bthd,bsd->bths', qi.astype(jnp.float32), k_idx.astype(jnp.float32))
    score = jnp.einsum('bth,bths->bts', wi.astype(jnp.float32) * INDEX_SCALE,
                       jax.nn.relu(logits))
    causal = jnp.arange(L, dtype=jnp.int32)[None, :] <= q_pos[:, None]
    score = jnp.where(causal[None], score, -jnp.inf)
    _, idx = lax.top_k(score, k_top)
    valid = idx <= q_pos[None, :, None]
    return idx, valid


def gathered_attention(q, k_sel, v_sel, valid):
    B, T, H, Dh = q.shape
    qg = q.reshape(B, T, N_KV_HEADS, H // N_KV_HEADS, Dh)
    s = jnp.einsum('btcgd,btscd->btcgs', qg, k_sel).astype(jnp.float32) * ATTN_SCALE
    s = jnp.where(valid[:, :, None, None, :], s, -jnp.inf)
    p = jax.nn.softmax(s, axis=-1).astype(v_sel.dtype)
    o = jnp.einsum('btcgs,btscd->btcgd', p, v_sel)
    return o.reshape(B, T, H * Dh)


def prompt_sparse_attention(q, k, v, qi, wi, k_idx):
    B, S = q.shape[:2]
    k_top = min(TOPK_MAX, S // 4)
    nb = S // QBLOCK

    def to_blocks(a):
        return jnp.moveaxis(a.reshape((B, nb, QBLOCK) + a.shape[2:]), 1, 0)

    pos = jnp.arange(S, dtype=jnp.int32).reshape(nb, QBLOCK)

    def one_block(args):
        q_b, qi_b, wi_b, pos_b = args
        idx, valid = indexer_topk(qi_b, wi_b, k_idx, pos_b, k_top)
        k_sel = jax.vmap(lambda kb, ib: kb[ib])(k, idx)
        v_sel = jax.vmap(lambda vb, ib: vb[ib])(v, idx)
        return gathered_attention(q_b, k_sel, v_sel, valid)

    out = lax.map(one_block, (to_blocks(q), to_blocks(qi), to_blocks(wi), pos))
    return jnp.moveaxis(out, 0, 1).reshape(B, S, N_HEADS * HEAD_DIM)


def sample_sparse_attention(q, k_new, v_new, qi, wi, ki_new,
                            cache_k, cache_v, cache_idx_k, page_table, layer):
    Bd, T = q.shape[:2]
    n_pages = page_table.shape[1]
    past = n_pages * PAGE_SIZE
    k_top = min(TOPK_MAX, (past + T) // 4)
    k_idx_past = cache_idx_k[layer, page_table].reshape(Bd, past, IDX_DIM)
    k_idx = jnp.concatenate([k_idx_past.astype(ki_new.dtype), ki_new], axis=1)
    pos = past + jnp.arange(T, dtype=jnp.int32)
    idx, valid = indexer_topk(qi, wi, k_idx, pos, k_top)
    is_past = idx < past
    phys = jax.vmap(lambda pt, pi: pt[pi])(page_table, jnp.minimum(idx // PAGE_SIZE, n_pages - 1))
    row = idx % PAGE_SIZE
    new_i = jnp.clip(idx - past, 0, T - 1)

    def gather(pool, new):
        from_pool = pool[layer, phys, row].astype(new.dtype)
        from_new = jax.vmap(lambda nb, ib: nb[ib])(new, new_i)
        return jnp.where(is_past[..., None, None], from_pool, from_new)

    return gathered_attention(q, gather(cache_k, k_new), gather(cache_v, v_new), valid)


def residual_update(x, conv_o, attn_o, w_out, g_post_mix, g_pre_ffn,
                    w_gate, w_up, w_down, g_post_ffn):
    mixed = jnp.concatenate([conv_o, attn_o], axis=-1) @ w_out
    x = x + rms_norm(mixed, g_post_mix)
    h = rms_norm(x, g_pre_ffn)
    f = (jax.nn.silu(h @ w_gate) * (h @ w_up)) @ w_down
    return x + rms_norm(f, g_post_ffn)


def setup_inputs(seed: int = 0) -> dict:
    key = jax.random.key(seed)
    ks = jax.random.split(key, 24)
    n_pages = PAST_LEN // PAGE_SIZE
    n_used = DEC_BATCH * n_pages
    n_pool = n_used + (n_used + 3) // 4
    nrm = jax.random.normal
    f32 = jnp.float32

    def gain(k, n):
        return 1.0 + 0.05 * nrm(k, (DEPTH, n), f32)

    page_table = jax.random.permutation(ks[0], n_pool)[:n_used].reshape(DEC_BATCH, n_pages).astype(jnp.int32)
    return {
        "x_prompt": nrm(ks[1], (BATCH, SEQ, D_MODEL), f32),
        "x_sample": nrm(ks[2], (DEC_BATCH, DEC_SEQ, D_MODEL), f32),
        "cache_k": nrm(ks[3], (DEPTH, n_pool, PAGE_SIZE, N_KV_HEADS, HEAD_DIM), f32),
        "cache_v": nrm(ks[4], (DEPTH, n_pool, PAGE_SIZE, N_KV_HEADS, HEAD_DIM), f32),
        "cache_idx_k": nrm(ks[5], (DEPTH, n_pool, PAGE_SIZE, IDX_DIM), f32),
        "state_conv": 0.5 * nrm(ks[6], (DEPTH, DEC_BATCH, CONV_WIDTH - 1, CONV_CH), f32),
        "page_table": page_table,
        "g_pre_mix": gain(ks[7], D_MODEL),
        "w_in": nrm(ks[8], (DEPTH, D_MODEL, N_IN), f32) * D_MODEL ** -0.5,
        "conv_w": nrm(ks[9], (DEPTH, CONV_WIDTH, CONV_CH), f32) * CONV_WIDTH ** -0.5,
        "conv_b": 0.02 * nrm(ks[10], (DEPTH, CONV_CH), f32),
        "conv_ln_g": gain(ks[11], CONV_CH),
        "conv_ln_b": 0.02 * nrm(ks[12], (DEPTH, CONV_CH), f32),
        "w_out": nrm(ks[13], (DEPTH, MIX_WIDTH, D_MODEL), f32) * MIX_WIDTH ** -0.5,
        "g_post_mix": gain(ks[14], D_MODEL),
        "g_pre_ffn": gain(ks[15], D_MODEL),
        "w_gate": nrm(ks[16], (DEPTH, D_MODEL, D_FF), f32) * D_MODEL ** -0.5,
        "w_up": nrm(ks[17], (DEPTH, D_MODEL, D_FF), f32) * D_MODEL ** -0.5,
        "w_down": nrm(ks[18], (DEPTH, D_FF, D_MODEL), f32) * D_FF ** -0.5,
        "g_post_ffn": gain(ks[19], D_MODEL),
    }


def reference(x_prompt, x_sample, cache_k, cache_v, cache_idx_k, state_conv, page_table,
              g_pre_mix, w_in, conv_w, conv_b, conv_ln_g, conv_ln_b, w_out,
              g_post_mix, g_pre_ffn, w_gate, w_up, w_down, g_post_ffn):
    xp, xs = x_prompt, x_sample
    kp_l, vp_l, ip_l, cp_l, ks_l, vs_l, is_l, cs_l = [], [], [], [], [], [], [], []
    for l in range(DEPTH):
        u_a, u_g, q, k, v, qi, ki, wi = project_in(xp, g_pre_mix[l], w_in[l])
        hist0 = jnp.zeros((xp.shape[0], CONV_WIDTH - 1, CONV_CH), xp.dtype)
        conv_o, conv_st = conformer_conv(u_a, u_g, hist0, conv_w[l], conv_b[l],
                                         conv_ln_g[l], conv_ln_b[l])
        attn_o = prompt_sparse_attention(q, k, v, qi, wi, ki)
        xp = residual_update(xp, conv_o, attn_o, w_out[l], g_post_mix[l], g_pre_ffn[l],
                             w_gate[l], w_up[l], w_down[l], g_post_ffn[l])
        kp_l.append(k); vp_l.append(v); ip_l.append(ki); cp_l.append(conv_st)

        u_a, u_g, q, k, v, qi, ki, wi = project_in(xs, g_pre_mix[l], w_in[l])
        conv_o, conv_st = conformer_conv(u_a, u_g, state_conv[l], conv_w[l], conv_b[l],
                                         conv_ln_g[l], conv_ln_b[l])
        attn_o = sample_sparse_attention(q, k, v, qi, wi, ki, cache_k, cache_v,
                                         cache_idx_k, page_table, l)
        xs = residual_update(xs, conv_o, attn_o, w_out[l], g_post_mix[l], g_pre_ffn[l],
                             w_gate[l], w_up[l], w_down[l], g_post_ffn[l])
        ks_l.append(k); vs_l.append(v); is_l.append(ki); cs_l.append(conv_st)

    return (xp, xs,
            jnp.stack(kp_l), jnp.stack(vp_l), jnp.stack(ip_l), jnp.stack(cp_l),
            jnp.stack(ks_l), jnp.stack(vs_l), jnp.stack(is_l), jnp.stack(cs_l))
```

```python
import functools

import jax
import jax.numpy as jnp
from jax import lax
from jax.experimental import pallas as pl
from jax.experimental.pallas import tpu as pltpu

N_HEADS = 8
N_KV_HEADS = 2
N_GROUP = N_HEADS // N_KV_HEADS
HEAD_DIM = 128
N_IDX_HEADS = 16
IDX_DIM = 64
TOPK_MAX = 256
EPS = 1e-6
LANES = 128
VMEM_LIMIT = 56 * 1024 * 1024

INT_MIN = -2 ** 31
NEG_INF_KEY = (0xFF800000 ^ 0x7FFFFFFF) - 2 ** 32
MASK_BIAS = -1e30

F32 = jnp.float32
BF16 = jnp.bfloat16
I32 = jnp.int32


def _dot(a, b):
    return jnp.dot(a, b, preferred_element_type=F32)


def _dot_nt(a, b):
    return lax.dot_general(a, b, (((1,), (1,)), ((), ())), preferred_element_type=F32)


def _sigmoid(x):
    return 1.0 / (1.0 + jnp.exp(-x))


def _rms(x, g):
    return x * lax.rsqrt(jnp.mean(x * x, axis=-1, keepdims=True) + EPS) * g


def _sort_key(x):
    b = pltpu.bitcast(x, I32)
    return b ^ ((b >> 31) & 0x7FFFFFFF)


def _params(*sem):
    return pltpu.CompilerParams(dimension_semantics=sem, vmem_limit_bytes=VMEM_LIMIT)


def _resident(shape):
    nd = len(shape)
    return pl.BlockSpec(shape, lambda *_: (0,) * nd, pipeline_mode=pl.Buffered(1))


def _proj_kernel(x_ref, g_ref, w_ref, u_ref, q_ref, k_ref, v_ref, kb_ref, vb_ref,
                 qi_ref, kw_ref, *, offs, attn_scale):
    h = _rms(x_ref[...], g_ref[...]).astype(BF16)
    o_ua, o_ug, o_q, o_k, o_v, o_qi, o_kw, o_end = offs
    ua = _dot(h, w_ref[:, o_ua:o_ug])
    ug = _dot(h, w_ref[:, o_ug:o_q])
    u_ref[...] = ua * _sigmoid(ug)
    q_ref[...] = (_dot(h, w_ref[:, o_q:o_k]) * attn_scale).astype(BF16)
    k = _dot(h, w_ref[:, o_k:o_v])
    k_ref[...] = k
    kb_ref[...] = k.astype(BF16)
    v = _dot(h, w_ref[:, o_v:o_qi])
    v_ref[...] = v
    vb_ref[...] = v.astype(BF16)
    qi_ref[...] = _dot(h, w_ref[:, o_qi:o_kw]).astype(BF16)
    kw_ref[...] = _dot(h, w_ref[:, o_kw:o_end])


def _proj(x, g, w, sizes, tm):
    T, D = x.shape
    c_conv, _, c_q, c_kv, _, c_qi, c_ki, c_wi = sizes
    offs = [0]
    for s in sizes[:6]:
        offs.append(offs[-1] + s)
    offs.append(w.shape[1])
    kw_w = offs[-1] - offs[-2]
    row = lambda n: pl.BlockSpec((tm, n), lambda i: (i, 0))
    out_shape = (
        jax.ShapeDtypeStruct((T, c_conv), F32),
        jax.ShapeDtypeStruct((T, c_q), BF16),
        jax.ShapeDtypeStruct((T, c_kv), F32),
        jax.ShapeDtypeStruct((T, c_kv), F32),
        jax.ShapeDtypeStruct((T, c_kv), BF16),
        jax.ShapeDtypeStruct((T, c_kv), BF16),
        jax.ShapeDtypeStruct((T, c_qi), BF16),
        jax.ShapeDtypeStruct((T, kw_w), F32),
    )
    return pl.pallas_call(
        functools.partial(_proj_kernel, offs=tuple(offs), attn_scale=HEAD_DIM ** -0.5),
        grid=(T // tm,),
        in_specs=[row(D), _resident((1, D)), _resident(w.shape)],
        out_specs=tuple(row(s.shape[1]) for s in out_shape),
        out_shape=out_shape,
        compiler_params=_params("parallel"),
        name="proj",
    )(x, g, w)


HIST_PAD = 32


def _conv_body(full_ref, y_ref, w_ref, b_ref, g_ref, beta_ref, o_ref, *, width, tb, cw):
    C = full_ref.shape[1]
    base = HIST_PAD - (width - 1)
    for c in range(C // cw):
        cs = slice(c * cw, (c + 1) * cw)
        acc = jnp.zeros((tb, cw), F32)
        for j in range(width):
            acc = acc + full_ref[base + j:base + j + tb, cs] * w_ref[j:j + 1, cs]
        y_ref[:, cs] = acc + b_ref[:, cs]
    y = y_ref[...]
    mu = jnp.mean(y, axis=-1, keepdims=True)
    yc = y - mu
    z = yc * lax.rsqrt(jnp.mean(yc * yc, axis=-1, keepdims=True) + EPS) * g_ref[...] + beta_ref[...]
    o_ref[...] = (z * _sigmoid(z)).astype(o_ref.dtype)


def _conv_prompt_kernel(prev_ref, u_ref, w_ref, b_ref, g_ref, beta_ref, o_ref,
                        full_ref, y_ref, *, width, tb, cw):
    @pl.when(pl.program_id(1) == 0)
    def _():
        full_ref[0:HIST_PAD, :] = jnp.zeros((HIST_PAD, full_ref.shape[1]), F32)

    @pl.when(pl.program_id(1) > 0)
    def _():
        full_ref[0:HIST_PAD, :] = prev_ref[...]

    full_ref[HIST_PAD:HIST_PAD + tb, :] = u_ref[...]
    _conv_body(full_ref, y_ref, w_ref, b_ref, g_ref, beta_ref, o_ref, width=width, tb=tb, cw=cw)


def _conv_prompt(u, conv_w, conv_b, ln_g, ln_b, tb):
    B, S, C = u.shape
    width = conv_w.shape[0]
    r = tb // HIST_PAD
    vec = lambda: _resident((1, C))
    return pl.pallas_call(
        functools.partial(_conv_prompt_kernel, width=width, tb=tb, cw=256),
        grid=(B, S // tb),
        in_specs=[
            pl.BlockSpec((None, HIST_PAD, C), lambda b, i: (b, jnp.maximum(i * r - 1, 0), 0)),
            pl.BlockSpec((None, tb, C), lambda b, i: (b, i, 0)),
            _resident(conv_w.shape), vec(), vec(), vec(),
        ],
        out_specs=pl.BlockSpec((None, tb, C), lambda b, i: (b, i, 0)),
        out_shape=jax.ShapeDtypeStruct((B, S, C), BF16),
        scratch_shapes=[pltpu.VMEM((HIST_PAD + tb, C), F32), pltpu.VMEM((tb, C), F32)],
        compiler_params=_params("parallel", "arbitrary"),
        name="conv_prompt",
    )(u, u, conv_w, conv_b, ln_g, ln_b)


def _conv_sample_kernel(hist_ref, u_ref, w_ref, b_ref, g_ref, beta_ref, o_ref, st_ref,
                        full_ref, y_ref, *, width, tb, cw):
    full_ref[0:HIST_PAD, :] = hist_ref[...]
    full_ref[HIST_PAD:HIST_PAD + tb, :] = u_ref[...]
    _conv_body(full_ref, y_ref, w_ref, b_ref, g_ref, beta_ref, o_ref, width=width, tb=tb, cw=cw)
    st_ref[...] = full_ref[HIST_PAD + tb - (width - 1):HIST_PAD + tb, :]


def _conv_sample(u, hist, conv_w, conv_b, ln_g, ln_b):
    B, T, C = u.shape
    width = conv_w.shape[0]
    hist_p = jnp.pad(hist, ((0, 0), (HIST_PAD - (width - 1), 0), (0, 0)))
    vec = lambda: _resident((1, C))
    return pl.pallas_call(
        functools.partial(_conv_sample_kernel, width=width, tb=T, cw=256),
        grid=(B,),
        in_specs=[
            pl.BlockSpec((None, HIST_PAD, C), lambda b: (b, 0, 0)),
            pl.BlockSpec((None, T, C), lambda b: (b, 0, 0)),
            _resident(conv_w.shape), vec(), vec(), vec(),
        ],
        out_specs=(pl.BlockSpec((None, T, C), lambda b: (b, 0, 0)),
                   pl.BlockSpec((None, width - 1, C), lambda b: (b, 0, 0))),
        out_shape=(jax.ShapeDtypeStruct((B, T, C), BF16),
                   jax.ShapeDtypeStruct((B, width - 1, C), F32)),
        scratch_shapes=[pltpu.VMEM((HIST_PAD + T, C), F32), pltpu.VMEM((T, C), F32)],
        compiler_params=_params("parallel"),
        name="conv_sample",
    )(hist_p, u, conv_w, conv_b, ln_g, ln_b)


def _kth_largest_key(count_ge, rows, k_top):
    def bit_step(it, ub):
        cand_u = ub | jnp.left_shift(jnp.int32(1), 31 - it)
        cnt = count_ge(cand_u ^ INT_MIN)
        return jnp.where(cnt >= k_top, cand_u, ub)

    ub = lax.fori_loop(0, 32, bit_step, jnp.zeros((rows, 1), I32))
    return ub ^ INT_MIN


def _count_blocks(keys, cand_b):
    acc = None
    for j in range(keys.shape[1] // LANES):
        hit = jnp.where(keys[:, j * LANES:(j + 1) * LANES] >= cand_b, 1, 0)
        acc = hit if acc is None else acc + hit
    return acc


def _lane_total(cnt):
    return jnp.sum(cnt.astype(F32), axis=-1, keepdims=True).astype(I32)


def _softmax_step(s, m_ref, l_ref, acc_ref, v, c):
    m_old = m_ref[c]
    m_new = jnp.maximum(m_old, jnp.max(s, axis=-1, keepdims=True))
    alpha = jnp.exp(m_old - m_new)
    p = jnp.exp(s - m_new)
    l_ref[c] = alpha * l_ref[c] + jnp.sum(p, axis=-1, keepdims=True)
    acc_ref[c] = alpha * acc_ref[c] + _dot(p.astype(BF16), v)
    m_ref[c] = m_new


def _attn_prompt_kernel(qi_ref, kw_ref, q_ref, ki_ref, k_ref, v_ref, o_ref,
                        keys_ref, qh_ref, wb_ref, m_ref, l_ref, acc_ref,
                        *, qb, sc, k_top, index_scale):
    i = pl.program_id(1)
    n_chunks = ((i + 1) * qb + sc - 1) // sc

    lane = lax.broadcasted_iota(I32, (qb, LANES), 1)
    for h in range(N_IDX_HEADS):
        blk = qi_ref[:, (h // 2) * LANES:(h // 2 + 1) * LANES]
        keep = (lane < IDX_DIM) if h % 2 == 0 else (lane >= IDX_DIM)
        qh_ref[h] = jnp.where(keep, blk, jnp.zeros_like(blk))
        w_h = kw_ref[:, IDX_DIM + h:IDX_DIM + h + 1] * index_scale
        wb_ref[h] = jnp.broadcast_to(w_h, (qb, LANES))

    q_pos = i * qb + lax.broadcasted_iota(I32, (qb, sc), 0)

    def score_chunk(c, carry):
        ks = ki_ref[pl.ds(pl.multiple_of(c * sc, sc), sc), :]
        parts = [jnp.zeros((qb, LANES), F32) for _ in range(sc // LANES)]
        for h in range(N_IDX_HEADS):
            logit = _dot_nt(qh_ref[h], ks)
            w_b = wb_ref[h]
            for j in range(sc // LANES):
                parts[j] = parts[j] + jnp.maximum(logit[:, j * LANES:(j + 1) * LANES], 0.0) * w_b
        score = jnp.concatenate(parts, axis=-1)
        s_pos = c * sc + lax.broadcasted_iota(I32, (qb, sc), 1)
        keys_ref[c] = jnp.where(s_pos <= q_pos, _sort_key(score), NEG_INF_KEY)
        return carry

    lax.fori_loop(0, n_chunks, score_chunk, 0)

    def count_ge(cand):
        cand_b = jnp.broadcast_to(cand, (qb, LANES))

        def body(c, cnt):
            return cnt + _count_blocks(keys_ref[c], cand_b)

        return _lane_total(lax.fori_loop(0, n_chunks, body, jnp.zeros((qb, LANES), I32)))

    thr = _kth_largest_key(count_ge, qb, k_top)
    thr = jnp.maximum(thr, NEG_INF_KEY + 1)
    thr_b = jnp.broadcast_to(thr, (qb, LANES))

    rows = N_GROUP * qb
    m_ref[...] = jnp.full(m_ref.shape, MASK_BIAS, F32)
    l_ref[...] = jnp.zeros(l_ref.shape, F32)
    acc_ref[...] = jnp.zeros(acc_ref.shape, F32)

    def attn_chunk(c, carry):
        keys = keys_ref[c]
        bias = jnp.concatenate(
            [jnp.where(keys[:, j * LANES:(j + 1) * LANES] >= thr_b, 0.0, MASK_BIAS)
             for j in range(sc // LANES)], axis=-1)
        bias = jnp.concatenate([bias] * N_GROUP, axis=0)
        start = pl.multiple_of(c * sc, sc)
        for kv in range(N_KV_HEADS):
            qc = jnp.concatenate(
                [q_ref[:, (kv * N_GROUP + g) * HEAD_DIM:(kv * N_GROUP + g + 1) * HEAD_DIM]
                 for g in range(N_GROUP)], axis=0)
            kc = k_ref[pl.ds(start, sc), kv * HEAD_DIM:(kv + 1) * HEAD_DIM]
            vc = v_ref[pl.ds(start, sc), kv * HEAD_DIM:(kv + 1) * HEAD_DIM]
            _softmax_step(_dot_nt(qc, kc) + bias, m_ref, l_ref, acc_ref, vc, kv)
        return carry

    lax.fori_loop(0, n_chunks, attn_chunk, 0)

    for kv in range(N_KV_HEADS):
        o = acc_ref[kv] / l_ref[kv]
        for g in range(N_GROUP):
            hd = kv * N_GROUP + g
            o_ref[:, hd * HEAD_DIM:(hd + 1) * HEAD_DIM] = o[g * qb:(g + 1) * qb, :].astype(o_ref.dtype)


def _attn_prompt(qi, kw, q, ki2, kb, vb, qb, sc):
    B, S, _ = q.shape
    k_top = min(TOPK_MAX, S // 4)
    assert sc >= k_top and S % sc == 0 and S % qb == 0 and sc % qb == 0
    qblk = lambda n: pl.BlockSpec((None, qb, n), lambda b, i: (b, i, 0))
    whole = lambda n: pl.BlockSpec((None, S, n), lambda b, i: (b, 0, 0))
    rows = N_GROUP * qb
    return pl.pallas_call(
        functools.partial(_attn_prompt_kernel, qb=qb, sc=sc, k_top=k_top,
                          index_scale=(N_IDX_HEADS * IDX_DIM) ** -0.5),
        grid=(B, S // qb),
        in_specs=[qblk(qi.shape[2]), qblk(kw.shape[2]), qblk(q.shape[2]),
                  whole(ki2.shape[2]), whole(kb.shape[2]), whole(vb.shape[2])],
        out_specs=qblk(q.shape[2]),
        out_shape=jax.ShapeDtypeStruct(q.shape, BF16),
        scratch_shapes=[
            pltpu.VMEM((S // sc, qb, sc), I32),
            pltpu.VMEM((N_IDX_HEADS, qb, LANES), BF16),
            pltpu.VMEM((N_IDX_HEADS, qb, LANES), F32),
            pltpu.VMEM((N_KV_HEADS, rows, 1), F32),
            pltpu.VMEM((N_KV_HEADS, rows, 1), F32),
            pltpu.VMEM((N_KV_HEADS, rows, HEAD_DIM), F32),
        ],
        compiler_params=_params("parallel", "arbitrary"),
        name="attn_prompt",
    )(qi, kw, q, ki2, kb, vb)


def _attn_sample_kernel(pt_ref, qi_ref, kw_ref, q_ref, kin_ref, kn_ref, vn_ref, *rest,
                        n_pg, n_steps, t_new, page, k_top, index_scale):
    ci_refs = rest[:n_pg]
    ck_refs = rest[n_pg:2 * n_pg]
    cv_refs = rest[2 * n_pg:3 * n_pg]
    o_ref = rest[3 * n_pg]
    keys_ref, keyn_ref, lhs_ref, wb_ref, thr_ref, ibuf, kbuf, vbuf, m_ref, l_ref, acc_ref = rest[3 * n_pg + 1:]
    j = pl.program_id(1)
    sc = n_pg * page
    rows_i = N_IDX_HEADS * t_new
    rows = N_GROUP * t_new

    def head_sum(x):
        return jnp.sum(x.reshape(N_IDX_HEADS, t_new, x.shape[-1]), axis=0)

    @pl.when(j == 0)
    def _():
        qi = qi_ref[...].astype(F32)
        lane = lax.broadcasted_iota(I32, (t_new, LANES), 1)
        pieces, wcols = [], []
        for h in range(N_IDX_HEADS):
            blk = qi[:, (h // 2) * LANES:(h // 2 + 1) * LANES]
            keep = (lane < IDX_DIM) if h % 2 == 0 else (lane >= IDX_DIM)
            pieces.append(jnp.where(keep, blk, 0.0))
            w_h = kw_ref[:, IDX_DIM + h:IDX_DIM + h + 1] * index_scale
            wcols.append(jnp.broadcast_to(w_h, (t_new, LANES)))
        lhs_ref[...] = jnp.concatenate(pieces, axis=0).astype(BF16)
        wb_ref[...] = jnp.concatenate(wcols, axis=0)

    def scores(key_mat):
        logit = _dot_nt(lhs_ref[...], key_mat)
        w_b = wb_ref[...]
        parts = [head_sum(jnp.maximum(logit[:, c * LANES:(c + 1) * LANES], 0.0) * w_b)
                 for c in range(key_mat.shape[0] // LANES)]
        return jnp.concatenate(parts, axis=-1)

    @pl.when(j < n_steps)
    def _():
        for p in range(n_pg):
            pg = ci_refs[p][...].astype(BF16)
            ibuf[p * page:(p + 1) * page, :] = jnp.concatenate([pg, pg], axis=-1)
        keys_ref[j] = _sort_key(scores(ibuf[...]))

    @pl.when(j == n_steps - 1)
    def _():
        s_new = scores(kin_ref[...])
        t_pos = lax.broadcasted_iota(I32, (t_new, LANES), 0)
        s_pos = lax.broadcasted_iota(I32, (t_new, LANES), 1)
        keyn_ref[...] = jnp.where(s_pos <= t_pos, _sort_key(s_new), NEG_INF_KEY)

        def count_ge(cand):
            cand_b = jnp.broadcast_to(cand, (t_new, LANES))

            def body(c, cnt):
                return cnt + _count_blocks(keys_ref[c], cand_b)

            cnt = lax.fori_loop(0, n_steps, body, _count_blocks(keyn_ref[...], cand_b))
            return _lane_total(cnt)

        thr = _kth_largest_key(count_ge, t_new, k_top)
        thr_ref[...] = jnp.broadcast_to(jnp.maximum(thr, NEG_INF_KEY + 1), (t_new, LANES))
        m_ref[...] = jnp.full(m_ref.shape, MASK_BIAS, F32)
        l_ref[...] = jnp.zeros(l_ref.shape, F32)
        acc_ref[...] = jnp.zeros(acc_ref.shape, F32)

    def attend(keys, k_mat, v_mat):
        thr_b = thr_ref[...]
        bias = jnp.concatenate(
            [jnp.where(keys[:, c * LANES:(c + 1) * LANES] >= thr_b, 0.0, MASK_BIAS)
             for c in range(keys.shape[1] // LANES)], axis=-1)
        bias = jnp.concatenate([bias] * N_GROUP, axis=0)
        q = q_ref[...].astype(F32)
        for kv in range(N_KV_HEADS):
            qc = jnp.concatenate(
                [q[:, (kv * N_GROUP + g) * HEAD_DIM:(kv * N_GROUP + g + 1) * HEAD_DIM]
                 for g in range(N_GROUP)], axis=0).astype(BF16)
            hs = slice(kv * HEAD_DIM, (kv + 1) * HEAD_DIM)
            _softmax_step(_dot_nt(qc, k_mat[:, hs]) + bias, m_ref, l_ref, acc_ref, v_mat[:, hs], kv)

    @pl.when(j >= n_steps)
    def _():
        for p in range(n_pg):
            kbuf[p * page:(p + 1) * page, :] = ck_refs[p][...].astype(BF16)
            vbuf[p * page:(p + 1) * page, :] = cv_refs[p][...].astype(BF16)
        attend(keys_ref[j - n_steps], kbuf[...], vbuf[...])

    @pl.when(j == 2 * n_steps - 1)
    def _():
        attend(keyn_ref[...], kn_ref[...], vn_ref[...])
        for kv in range(N_KV_HEADS):
            o = acc_ref[kv] / l_ref[kv]
            for g in range(N_GROUP):
                hd = kv * N_GROUP + g
                o_ref[:, hd * HEAD_DIM:(hd + 1) * HEAD_DIM] = (
                    o[g * t_new:(g + 1) * t_new, :].astype(o_ref.dtype))


def _attn_sample(page_table, qi, kw, q, ki2n, kbn, vbn, cache_ik, cache_k, cache_v, n_pg):
    B, T, _ = q.shape
    n_pages = page_table.shape[1]
    page = cache_k.shape[1]
    assert n_pages % n_pg == 0 and page == LANES
    n_steps = n_pages // n_pg
    sc = n_pg * page
    k_top = min(TOPK_MAX, (n_pages * page + T) // 4)
    rows = N_GROUP * T

    per_b = lambda a: pl.BlockSpec((None,) + a.shape[1:], lambda b, j, pt: (b, 0, 0))

    def idx_page(p):
        return pl.BlockSpec((None, page, cache_ik.shape[2]),
                            lambda b, j, pt: (pt[b, jnp.minimum(j, n_steps - 1) * n_pg + p], 0, 0))

    def kv_page(a, p):
        return pl.BlockSpec((None, page, a.shape[2]),
                            lambda b, j, pt: (pt[b, jnp.maximum(j - n_steps, 0) * n_pg + p], 0, 0))

    in_specs = ([per_b(qi), per_b(kw), per_b(q), per_b(ki2n), per_b(kbn), per_b(vbn)]
                + [idx_page(p) for p in range(n_pg)]
                + [kv_page(cache_k, p) for p in range(n_pg)]
                + [kv_page(cache_v, p) for p in range(n_pg)])
    grid_spec = pltpu.PrefetchScalarGridSpec(
        num_scalar_prefetch=1,
        grid=(B, 2 * n_steps),
        in_specs=in_specs,
        out_specs=pl.BlockSpec((None, T, q.shape[2]), lambda b, j, pt: (b, 0, 0)),
        scratch_shapes=[
            pltpu.VMEM((n_steps, T, sc), I32),
            pltpu.VMEM((T, LANES), I32),
            pltpu.VMEM((N_IDX_HEADS * T, LANES), BF16),
            pltpu.VMEM((N_IDX_HEADS * T, LANES), F32),
            pltpu.VMEM((T, LANES), I32),
            pltpu.VMEM((sc, LANES), BF16),
            pltpu.VMEM((sc, cache_k.shape[2]), BF16),
            pltpu.VMEM((sc, cache_v.shape[2]), BF16),
            pltpu.VMEM((N_KV_HEADS, rows, 1), F32),
            pltpu.VMEM((N_KV_HEADS, rows, 1), F32),
            pltpu.VMEM((N_KV_HEADS, rows, HEAD_DIM), F32),
        ],
    )
    return pl.pallas_call(
        functools.partial(_attn_sample_kernel, n_pg=n_pg, n_steps=n_steps, t_new=T, page=page,
                          k_top=k_top, index_scale=(N_IDX_HEADS * IDX_DIM) ** -0.5),
        grid_spec=grid_spec,
        out_shape=jax.ShapeDtypeStruct(q.shape, BF16),
        compiler_params=_params("parallel", "arbitrary"),
        name="attn_sample",
    )(page_table, qi, kw, q, ki2n, kbn, vbn,
      *([cache_ik] * n_pg), *([cache_k] * n_pg), *([cache_v] * n_pg))


def _mix_kernel(x_ref, c_ref, a_ref, w_ref, g1_ref, g2_ref, x1_ref, h_ref, *, c_conv):
    mixed = _dot(c_ref[...], w_ref[0:c_conv, :]) + _dot(a_ref[...], w_ref[c_conv:, :])
    x1 = x_ref[...] + _rms(mixed, g1_ref[...])
    x1_ref[...] = x1
    h_ref[...] = _rms(x1, g2_ref[...]).astype(BF16)


def _mix(x, conv_o, attn_o, w_out, g_post, g_pre, tm):
    T, D = x.shape
    row = lambda n: pl.BlockSpec((tm, n), lambda i: (i, 0))
    return pl.pallas_call(
        functools.partial(_mix_kernel, c_conv=conv_o.shape[1]),
        grid=(T // tm,),
        in_specs=[row(D), row(conv_o.shape[1]), row(attn_o.shape[1]),
                  _resident(w_out.shape), _resident((1, D)), _resident((1, D))],
        out_specs=(row(D), row(D)),
        out_shape=(jax.ShapeDtypeStruct((T, D), F32), jax.ShapeDtypeStruct((T, D), BF16)),
        compiler_params=_params("parallel"),
        name="mix",
    )(x, conv_o, attn_o, w_out, g_post, g_pre)


def _ffn_kernel(x_ref, h_ref, wg_ref, wu_ref, wd_ref, g_ref, y_ref, acc_ref):
    j = pl.program_id(1)
    h = h_ref[...]
    gate = _dot(h, wg_ref[...])
    a = (gate * _sigmoid(gate) * _dot(h, wu_ref[...])).astype(BF16)
    part = _dot(a, wd_ref[...])

    @pl.when(j == 0)
    def _():
        acc_ref[...] = part

    @pl.when(j > 0)
    def _():
        acc_ref[...] += part

    @pl.when(j == pl.num_programs(1) - 1)
    def _():
        y_ref[...] = x_ref[...] + _rms(acc_ref[...], g_ref[...])


def _ffn(x1, h, w_gate, w_up, w_down, g_post, tm, tf):
    T, D = x1.shape
    FF = w_gate.shape[1]
    assert FF % tf == 0
    row = lambda: pl.BlockSpec((tm, D), lambda i, j: (i, 0))
    return pl.pallas_call(
        _ffn_kernel,
        grid=(T // tm, FF // tf),
        in_specs=[row(), row(),
                  pl.BlockSpec((D, tf), lambda i, j: (0, j)),
                  pl.BlockSpec((D, tf), lambda i, j: (0, j)),
                  pl.BlockSpec((tf, D), lambda i, j: (j, 0)),
                  _resident((1, D))],
        out_specs=row(),
        out_shape=jax.ShapeDtypeStruct((T, D), F32),
        scratch_shapes=[pltpu.VMEM((tm, D), F32)],
        compiler_params=_params("parallel", "arbitrary"),
        name="ffn",
    )(x1, h, w_gate, w_up, w_down, g_post)


def _tile(n, pref):
    t = min(pref, n)
    while n % t:
        t //= 2
    return t


def _ffn_tile(ff):
    for t in (512, 256, 128):
        if ff % t == 0:
            return t
    return ff


def _dup_lanes(ki):
    return jnp.concatenate([ki, ki], axis=-1).astype(BF16)


def kernel(x_prompt, x_sample, cache_k, cache_v, cache_idx_k, state_conv, page_table,
           g_pre_mix, w_in, conv_w, conv_b, conv_ln_g, conv_ln_b, w_out,
           g_post_mix, g_pre_ffn, w_gate, w_up, w_down, g_post_ffn):
    B, S, D = x_prompt.shape
    Bd, Td, _ = x_sample.shape
    depth = w_in.shape[0]
    c_conv = conv_w.shape[2]
    c_q = N_HEADS * HEAD_DIM
    c_kv = N_KV_HEADS * HEAD_DIM
    c_qi = N_IDX_HEADS * IDX_DIM
    sizes = (c_conv, c_conv, c_q, c_kv, c_kv, c_qi, IDX_DIM, N_IDX_HEADS)
    n_in = sum(sizes)
    assert w_in.shape[2] == n_in
    n_pad = -n_in % LANES
    n_pool, page = cache_k.shape[1], cache_k.shape[2]

    xp = x_prompt.reshape(B * S, D)
    xs = x_sample.reshape(Bd * Td, D)
    outs = [[] for _ in range(8)]
    row2 = lambda a: a.reshape(1, -1)

    for l in range(depth):
        w_in_b = jnp.pad(w_in[l], ((0, 0), (0, n_pad))).astype(BF16)
        w_out_b = w_out[l].astype(BF16)
        wg_b, wu_b, wd_b = w_gate[l].astype(BF16), w_up[l].astype(BF16), w_down[l].astype(BF16)
        g_pre, g_post, g_ffn_pre, g_ffn_post = (row2(g_pre_mix[l]), row2(g_post_mix[l]),
                                                row2(g_pre_ffn[l]), row2(g_post_ffn[l]))
        cw, cb, lg, lb = conv_w[l], row2(conv_b[l]), row2(conv_ln_g[l]), row2(conv_ln_b[l])
        tf = _ffn_tile(wg_b.shape[1])

        tm = _tile(B * S, 512)
        u, q, k, v, kb, vb, qi, kw = _proj(xp, g_pre, w_in_b, sizes, tm)
        r3 = lambda a: a.reshape(B, S, a.shape[-1])
        conv_o = _conv_prompt(r3(u), cw, cb, lg, lb, _tile(S, 128))
        ki = kw[:, :IDX_DIM]
        attn_o = _attn_prompt(r3(qi), r3(kw), r3(q), r3(_dup_lanes(ki)), r3(kb), r3(vb),
                              qb=_tile(S, 128), sc=_tile(S, 512))
        x1, h2 = _mix(xp, conv_o.reshape(B * S, c_conv), attn_o.reshape(B * S, c_q),
                      w_out_b, g_post, g_ffn_pre, tm)
        xp = _ffn(x1, h2, wg_b, wu_b, wd_b, g_ffn_post, tm, tf)
        outs[0].append(k.reshape(B, S, N_KV_HEADS, HEAD_DIM))
        outs[1].append(v.reshape(B, S, N_KV_HEADS, HEAD_DIM))
        outs[2].append(ki.reshape(B, S, IDX_DIM))
        outs[3].append(r3(u)[:, S - (cw.shape[0] - 1):, :])

        tms = _tile(Bd * Td, 256)
        u, q, k, v, kb, vb, qi, kw = _proj(xs, g_pre, w_in_b, sizes, tms)
        r3 = lambda a: a.reshape(Bd, Td, a.shape[-1])
        conv_o, conv_st = _conv_sample(r3(u), state_conv[l], cw, cb, lg, lb)
        ki = kw[:, :IDX_DIM]
        pad_rows = lambda a: jnp.pad(r3(a), ((0, 0), (0, LANES - Td), (0, 0)))
        attn_o = _attn_sample(
            page_table, r3(qi), r3(kw), r3(q), pad_rows(_dup_lanes(ki)), pad_rows(kb), pad_rows(vb),
            cache_idx_k[l], cache_k[l].reshape(n_pool, page, c_kv), cache_v[l].reshape(n_pool, page, c_kv),
            n_pg=_tile(page_table.shape[1], 8))
        x1, h2 = _mix(xs, conv_o.reshape(Bd * Td, c_conv), attn_o.reshape(Bd * Td, c_q),
                      w_out_b, g_post, g_ffn_pre, tms)
        xs = _ffn(x1, h2, wg_b, wu_b, wd_b, g_ffn_post, tms, tf)
        outs[4].append(k.reshape(Bd, Td, N_KV_HEADS, HEAD_DIM))
        outs[5].append(v.reshape(Bd, Td, N_KV_HEADS, HEAD_DIM))
        outs[6].append(ki.reshape(Bd, Td, IDX_DIM))
        outs[7].append(conv_st)

    return (xp.reshape(B, S, D), xs.reshape(Bd, Td, D), *[jnp.stack(o) for o in outs])
```

```python
import functools

import jax
import jax.numpy as jnp
from jax import lax
from jax.experimental import pallas as pl
from jax.experimental.pallas import tpu as pltpu

N_HEADS = 8
N_KV_HEADS = 2
N_GROUP = N_HEADS // N_KV_HEADS
HEAD_DIM = 128
N_IDX_HEADS = 16
IDX_DIM = 64
TOPK_MAX = 256
EPS = 1e-6
LANES = 128
SUBLANES = 8
VMEM_LIMIT = 56 * 1024 * 1024

INT_MIN = -2 ** 31
NEG_INF_KEY = (0xFF800000 ^ 0x7FFFFFFF) - 2 ** 32
MASK_BIAS = -1e30
LOG2_E = 1.4426950408889634

F32 = jnp.float32
BF16 = jnp.bfloat16
I32 = jnp.int32


def _dot(a, b):
    return jnp.dot(a, b, preferred_element_type=F32)


def _dot_nt(a, b):
    return lax.dot_general(a, b, (((1,), (1,)), ((), ())), preferred_element_type=F32)


def _sigmoid(x):
    return 1.0 / (1.0 + jnp.exp(-x))


def _rms(x, g):
    return x * lax.rsqrt(jnp.mean(x * x, axis=-1, keepdims=True) + EPS) * g


def _sort_key(x):
    b = pltpu.bitcast(x, I32)
    return b ^ ((b >> 31) & 0x7FFFFFFF)


def _params(*sem):
    return pltpu.CompilerParams(dimension_semantics=sem, vmem_limit_bytes=VMEM_LIMIT)


def _resident(shape):
    nd = len(shape)
    return pl.BlockSpec(shape, lambda *_: (0,) * nd, pipeline_mode=pl.Buffered(1))


def _proj_kernel(x_ref, g_ref, w_ref, u_ref, q_ref, k_ref, v_ref, kb_ref, vb_ref,
                 qi_ref, kw_ref, *, offs, attn_scale):
    h = _rms(x_ref[...], g_ref[...]).astype(BF16)
    o_ua, o_ug, o_q, o_k, o_v, o_qi, o_kw, o_end = offs
    ua = _dot(h, w_ref[:, o_ua:o_ug])
    ug = _dot(h, w_ref[:, o_ug:o_q])
    u_ref[...] = ua * _sigmoid(ug)
    q_ref[...] = (_dot(h, w_ref[:, o_q:o_k]) * attn_scale).astype(BF16)
    k = _dot(h, w_ref[:, o_k:o_v])
    k_ref[...] = k
    kb_ref[...] = k.astype(BF16)
    v = _dot(h, w_ref[:, o_v:o_qi])
    v_ref[...] = v
    vb_ref[...] = v.astype(BF16)
    qi_ref[...] = _dot(h, w_ref[:, o_qi:o_kw]).astype(BF16)
    kw_ref[...] = _dot(h, w_ref[:, o_kw:o_end])


def _proj(x, g, w, sizes, tm):
    T, D = x.shape
    c_conv, _, c_q, c_kv, _, c_qi, c_ki, c_wi = sizes
    offs = [0]
    for s in sizes[:6]:
        offs.append(offs[-1] + s)
    offs.append(w.shape[1])
    kw_w = offs[-1] - offs[-2]
    row = lambda n: pl.BlockSpec((tm, n), lambda i: (i, 0))
    out_shape = (
        jax.ShapeDtypeStruct((T, c_conv), F32),
        jax.ShapeDtypeStruct((T, c_q), BF16),
        jax.ShapeDtypeStruct((T, c_kv), F32),
        jax.ShapeDtypeStruct((T, c_kv), F32),
        jax.ShapeDtypeStruct((T, c_kv), BF16),
        jax.ShapeDtypeStruct((T, c_kv), BF16),
        jax.ShapeDtypeStruct((T, c_qi), BF16),
        jax.ShapeDtypeStruct((T, kw_w), F32),
    )
    return pl.pallas_call(
        functools.partial(_proj_kernel, offs=tuple(offs), attn_scale=HEAD_DIM ** -0.5 * LOG2_E),
        grid=(T // tm,),
        in_specs=[row(D), _resident((1, D)), _resident(w.shape)],
        out_specs=tuple(row(s.shape[1]) for s in out_shape),
        out_shape=out_shape,
        compiler_params=_params("parallel"),
        name="proj",
    )(x, g, w)


HIST_PAD = 32


def _conv_body(full_ref, sh_ref, y_ref, w_ref, b_ref, g_ref, beta_ref, o_ref, *, width, tb, cw):
    C = full_ref.shape[1]
    base = HIST_PAD - (width - 1)
    n_b = min(SUBLANES, width)
    n_a = [(width - 1 - b) // SUBLANES + 1 for b in range(n_b)]
    for b in range(n_b):
        rows_b = tb + SUBLANES * (n_a[b] - 1)
        sh_ref[b, 0:rows_b, :] = full_ref[base + b:base + b + rows_b, :]
    for c in range(C // cw):
        cs = slice(c * cw, (c + 1) * cw)
        acc = jnp.zeros((tb, cw), F32)
        for b in range(n_b):
            for a in range(n_a[b]):
                j = b + SUBLANES * a
                acc = acc + sh_ref[b, SUBLANES * a:SUBLANES * a + tb, cs] * w_ref[j:j + 1, cs]
        y_ref[:, cs] = acc + b_ref[:, cs]
    y = y_ref[...]
    mu = jnp.mean(y, axis=-1, keepdims=True)
    yc = y - mu
    z = yc * lax.rsqrt(jnp.mean(yc * yc, axis=-1, keepdims=True) + EPS) * g_ref[...] + beta_ref[...]
    o_ref[...] = (z * _sigmoid(z)).astype(o_ref.dtype)


def _conv_scratch(tb, C, width):
    n_b = min(SUBLANES, width)
    return [pltpu.VMEM((HIST_PAD + tb, C), F32),
            pltpu.VMEM((n_b, tb + SUBLANES * ((width - 1) // SUBLANES), C), F32),
            pltpu.VMEM((tb, C), F32)]


def _conv_prompt_kernel(prev_ref, u_ref, w_ref, b_ref, g_ref, beta_ref, o_ref,
                        full_ref, sh_ref, y_ref, *, width, tb, cw):
    @pl.when(pl.program_id(1) == 0)
    def _():
        full_ref[0:HIST_PAD, :] = jnp.zeros((HIST_PAD, full_ref.shape[1]), F32)

    @pl.when(pl.program_id(1) > 0)
    def _():
        full_ref[0:HIST_PAD, :] = prev_ref[...]

    full_ref[HIST_PAD:HIST_PAD + tb, :] = u_ref[...]
    _conv_body(full_ref, sh_ref, y_ref, w_ref, b_ref, g_ref, beta_ref, o_ref, width=width, tb=tb, cw=cw)


def _conv_prompt(u, conv_w, conv_b, ln_g, ln_b, tb):
    B, S, C = u.shape
    width = conv_w.shape[0]
    r = tb // HIST_PAD
    vec = lambda: _resident((1, C))
    return pl.pallas_call(
        functools.partial(_conv_prompt_kernel, width=width, tb=tb, cw=128),
        grid=(B, S // tb),
        in_specs=[
            pl.BlockSpec((None, HIST_PAD, C), lambda b, i: (b, jnp.maximum(i * r - 1, 0), 0)),
            pl.BlockSpec((None, tb, C), lambda b, i: (b, i, 0)),
            _resident(conv_w.shape), vec(), vec(), vec(),
        ],
        out_specs=pl.BlockSpec((None, tb, C), lambda b, i: (b, i, 0)),
        out_shape=jax.ShapeDtypeStruct((B, S, C), BF16),
        scratch_shapes=_conv_scratch(tb, C, width),
        compiler_params=_params("parallel", "arbitrary"),
        name="conv_prompt",
    )(u, u, conv_w, conv_b, ln_g, ln_b)


def _conv_sample_kernel(hist_ref, u_ref, w_ref, b_ref, g_ref, beta_ref, o_ref, st_ref,
                        full_ref, sh_ref, y_ref, *, width, tb, cw):
    full_ref[0:HIST_PAD, :] = hist_ref[...]
    full_ref[HIST_PAD:HIST_PAD + tb, :] = u_ref[...]
    _conv_body(full_ref, sh_ref, y_ref, w_ref, b_ref, g_ref, beta_ref, o_ref, width=width, tb=tb, cw=cw)
    st_ref[...] = full_ref[HIST_PAD + tb - (width - 1):HIST_PAD + tb, :]


def _conv_sample(u, hist, conv_w, conv_b, ln_g, ln_b):
    B, T, C = u.shape
    width = conv_w.shape[0]
    hist_p = jnp.pad(hist, ((0, 0), (HIST_PAD - (width - 1), 0), (0, 0)))
    vec = lambda: _resident((1, C))
    return pl.pallas_call(
        functools.partial(_conv_sample_kernel, width=width, tb=T, cw=256),
        grid=(B,),
        in_specs=[
            pl.BlockSpec((None, HIST_PAD, C), lambda b: (b, 0, 0)),
            pl.BlockSpec((None, T, C), lambda b: (b, 0, 0)),
            _resident(conv_w.shape), vec(), vec(), vec(),
        ],
        out_specs=(pl.BlockSpec((None, T, C), lambda b: (b, 0, 0)),
                   pl.BlockSpec((None, width - 1, C), lambda b: (b, 0, 0))),
        out_shape=(jax.ShapeDtypeStruct((B, T, C), BF16),
                   jax.ShapeDtypeStruct((B, width - 1, C), F32)),
        scratch_shapes=_conv_scratch(T, C, width),
        compiler_params=_params("parallel"),
        name="conv_sample",
    )(hist_p, u, conv_w, conv_b, ln_g, ln_b)


def _count(fold, rows, pred):
    def add_block(cnt, keys, pos0):
        for j in range(keys.shape[1] // LANES):
            hit = pred(keys[:, j * LANES:(j + 1) * LANES], pos0 + j * LANES)
            cnt = cnt + jnp.where(hit, 1, 0)
        return cnt

    cnt = fold(add_block, jnp.zeros((rows, LANES), I32))
    return jnp.sum(cnt.astype(F32), axis=-1, keepdims=True).astype(I32)


def _topk_threshold(fold, update, rows, k_top, n_pos_bits):
    def bcast(x):
        return jnp.broadcast_to(x, (rows, LANES))

    def count_ge(cand):
        cand_b = bcast(cand)
        return _count(fold, rows, lambda keys, pos0: keys >= cand_b)

    def bit_step(it, ub):
        cand_u = ub | jnp.left_shift(jnp.int32(1), 31 - it)
        return jnp.where(count_ge(cand_u ^ INT_MIN) >= k_top, cand_u, ub)

    thr = lax.fori_loop(0, 32, bit_step, jnp.zeros((rows, 1), I32)) ^ INT_MIN
    tie = (count_ge(thr) > k_top) & (thr > NEG_INF_KEY)

    @pl.when(jnp.sum(jnp.where(tie, 1.0, 0.0)) > 0.0)
    def _():
        need = k_top - count_ge(thr + 1)
        thr_b = bcast(thr)
        lane = lax.broadcasted_iota(I32, (rows, LANES), 1)

        def tied_before(cut):
            cut_b = bcast(cut)
            return _count(fold, rows, lambda keys, pos0: (keys == thr_b) & (lane + pos0 < cut_b))

        def cut_step(it, cut):
            cand = cut | jnp.left_shift(jnp.int32(1), n_pos_bits - 1 - it)
            return jnp.where(tied_before(cand) <= need, cand, cut)

        cut = lax.fori_loop(0, n_pos_bits, cut_step, jnp.zeros((rows, 1), I32))
        cut_b = bcast(jnp.where(tie, cut, 2 ** n_pos_bits))

        def drop(keys, pos0):
            parts = []
            for j in range(keys.shape[1] // LANES):
                blk = keys[:, j * LANES:(j + 1) * LANES]
                surplus = (blk == thr_b) & (lane + (pos0 + j * LANES) >= cut_b)
                parts.append(jnp.where(surplus, NEG_INF_KEY, blk))
            return jnp.concatenate(parts, axis=-1)

        update(drop)

    return jnp.maximum(thr, NEG_INF_KEY + 1)


def _softmax_step(s, m_ref, l_ref, acc_ref, v, c):
    m_old = m_ref[c]
    m_new = jnp.maximum(m_old, jnp.max(s, axis=-1, keepdims=True))
    alpha = jnp.exp2(m_old - m_new)
    p = jnp.exp2(s - jnp.tile(m_new, (1, s.shape[1] // LANES)))
    l_ref[c] = alpha * l_ref[c] + jnp.sum(p, axis=-1, keepdims=True)
    acc_ref[c] = alpha * acc_ref[c] + _dot(p.astype(BF16), v)
    m_ref[c] = m_new


def _attn_prompt_kernel(qi_ref, kw_ref, q_ref, ki_ref, k_ref, v_ref, o_ref,
                        keys_ref, qh_ref, wb_ref, m_ref, l_ref, acc_ref,
                        *, qb, sc, k_top, index_scale):
    i = pl.program_id(1)
    n_pos_bits = (keys_ref.shape[0] * sc).bit_length()
    n_chunks = ((i + 1) * qb + sc - 1) // sc

    lane = lax.broadcasted_iota(I32, (qb, LANES), 1)
    for h in range(N_IDX_HEADS):
        blk = qi_ref[:, (h // 2) * LANES:(h // 2 + 1) * LANES]
        keep = (lane < IDX_DIM) if h % 2 == 0 else (lane >= IDX_DIM)
        qh_ref[h] = jnp.where(keep, blk, jnp.zeros_like(blk))
        w_h = kw_ref[:, IDX_DIM + h:IDX_DIM + h + 1] * index_scale
        wb_ref[h] = jnp.broadcast_to(w_h, (qb, LANES))

    q_pos = i * qb + lax.broadcasted_iota(I32, (qb, sc), 0)

    def score_chunk(c, carry):
        ks = ki_ref[pl.ds(pl.multiple_of(c * sc, sc), sc), :]
        parts = [jnp.zeros((qb, LANES), F32) for _ in range(sc // LANES)]
        for h in range(N_IDX_HEADS):
            logit = _dot_nt(qh_ref[h], ks)
            w_b = wb_ref[h]
            for j in range(sc // LANES):
                parts[j] = parts[j] + jnp.maximum(logit[:, j * LANES:(j + 1) * LANES], 0.0) * w_b
        score = jnp.concatenate(parts, axis=-1)
        s_pos = c * sc + lax.broadcasted_iota(I32, (qb, sc), 1)
        keys_ref[c] = jnp.where(s_pos <= q_pos, _sort_key(score), NEG_INF_KEY)
        return carry

    lax.fori_loop(0, n_chunks, score_chunk, 0)

    def fold(f, init):
        return lax.fori_loop(0, n_chunks, lambda c, carry: f(carry, keys_ref[c], c * sc), init)

    def update(f):
        def body(c, carry):
            keys_ref[c] = f(keys_ref[c], c * sc)
            return carry
        lax.fori_loop(0, n_chunks, body, 0)

    thr = _topk_threshold(fold, update, qb, k_top, n_pos_bits)
    thr_b = jnp.broadcast_to(thr, (qb, LANES))

    rows = N_GROUP * qb
    m_ref[...] = jnp.full(m_ref.shape, MASK_BIAS, F32)
    l_ref[...] = jnp.zeros(l_ref.shape, F32)
    acc_ref[...] = jnp.zeros(acc_ref.shape, F32)

    def attn_chunk(c, carry):
        keys = keys_ref[c]
        bias = jnp.concatenate(
            [jnp.where(keys[:, j * LANES:(j + 1) * LANES] >= thr_b, 0.0, MASK_BIAS)
             for j in range(sc // LANES)], axis=-1)
        bias = jnp.concatenate([bias] * N_GROUP, axis=0)
        start = pl.multiple_of(c * sc, sc)
        for kv in range(N_KV_HEADS):
            qc = jnp.concatenate(
                [q_ref[:, (kv * N_GROUP + g) * HEAD_DIM:(kv * N_GROUP + g + 1) * HEAD_DIM]
                 for g in range(N_GROUP)], axis=0)
            kc = k_ref[pl.ds(start, sc), kv * HEAD_DIM:(kv + 1) * HEAD_DIM]
            vc = v_ref[pl.ds(start, sc), kv * HEAD_DIM:(kv + 1) * HEAD_DIM]
            _softmax_step(_dot_nt(qc, kc) + bias, m_ref, l_ref, acc_ref, vc, kv)
        return carry

    lax.fori_loop(0, n_chunks, attn_chunk, 0)

    for kv in range(N_KV_HEADS):
        o = acc_ref[kv] / l_ref[kv]
        for g in range(N_GROUP):
            hd = kv * N_GROUP + g
            o_ref[:, hd * HEAD_DIM:(hd + 1) * HEAD_DIM] = o[g * qb:(g + 1) * qb, :].astype(o_ref.dtype)


def _attn_prompt(qi, kw, q, ki2, kb, vb, qb, sc):
    B, S, _ = q.shape
    k_top = min(TOPK_MAX, S // 4)
    assert sc >= k_top and S % sc == 0 and S % qb == 0 and sc % qb == 0
    qblk = lambda n: pl.BlockSpec((None, qb, n), lambda b, i: (b, i, 0))
    whole = lambda n: pl.BlockSpec((None, S, n), lambda b, i: (b, 0, 0))
    rows = N_GROUP * qb
    return pl.pallas_call(
        functools.partial(_attn_prompt_kernel, qb=qb, sc=sc, k_top=k_top,
                          index_scale=(N_IDX_HEADS * IDX_DIM) ** -0.5),
        grid=(B, S // qb),
        in_specs=[qblk(qi.shape[2]), qblk(kw.shape[2]), qblk(q.shape[2]),
                  whole(ki2.shape[2]), whole(kb.shape[2]), whole(vb.shape[2])],
        out_specs=qblk(q.shape[2]),
        out_shape=jax.ShapeDtypeStruct(q.shape, BF16),
        scratch_shapes=[
            pltpu.VMEM((S // sc, qb, sc), I32),
            pltpu.VMEM((N_IDX_HEADS, qb, LANES), BF16),
            pltpu.VMEM((N_IDX_HEADS, qb, LANES), F32),
            pltpu.VMEM((N_KV_HEADS, rows, LANES), F32),
            pltpu.VMEM((N_KV_HEADS, rows, LANES), F32),
            pltpu.VMEM((N_KV_HEADS, rows, HEAD_DIM), F32),
        ],
        compiler_params=_params("parallel", "arbitrary"),
        name="attn_prompt",
    )(qi, kw, q, ki2, kb, vb)


def _attn_sample_kernel(pt_ref, qi_ref, kw_ref, q_ref, kin_ref, kn_ref, vn_ref, *rest,
                        n_pg, n_steps, t_new, page, k_top, index_scale):
    ci_refs = rest[:n_pg]
    ck_refs = rest[n_pg:2 * n_pg]
    cv_refs = rest[2 * n_pg:3 * n_pg]
    o_ref = rest[3 * n_pg]
    keys_ref, keyn_ref, lhs_ref, wb_ref, thr_ref, ibuf, kbuf, vbuf, m_ref, l_ref, acc_ref = rest[3 * n_pg + 1:]
    j = pl.program_id(1)
    sc = n_pg * page
    rows_i = N_IDX_HEADS * t_new
    rows = N_GROUP * t_new

    def head_sum(x):
        return jnp.sum(x.reshape(N_IDX_HEADS, t_new, x.shape[-1]), axis=0)

    @pl.when(j == 0)
    def _():
        qi = qi_ref[...].astype(F32)
        lane = lax.broadcasted_iota(I32, (t_new, LANES), 1)
        pieces, wcols = [], []
        for h in range(N_IDX_HEADS):
            blk = qi[:, (h // 2) * LANES:(h // 2 + 1) * LANES]
            keep = (lane < IDX_DIM) if h % 2 == 0 else (lane >= IDX_DIM)
            pieces.append(jnp.where(keep, blk, 0.0))
            w_h = kw_ref[:, IDX_DIM + h:IDX_DIM + h + 1] * index_scale
            wcols.append(jnp.broadcast_to(w_h, (t_new, LANES)))
        lhs_ref[...] = jnp.concatenate(pieces, axis=0).astype(BF16)
        wb_ref[...] = jnp.concatenate(wcols, axis=0)

    def scores(key_mat):
        logit = _dot_nt(lhs_ref[...], key_mat)
        w_b = wb_ref[...]
        parts = [head_sum(jnp.maximum(logit[:, c * LANES:(c + 1) * LANES], 0.0) * w_b)
                 for c in range(key_mat.shape[0] // LANES)]
        return jnp.concatenate(parts, axis=-1)

    @pl.when(j < n_steps)
    def _():
        for p in range(n_pg):
            pg = ci_refs[p][...].astype(BF16)
            ibuf[p * page:(p + 1) * page, :] = jnp.concatenate([pg, pg], axis=-1)
        keys_ref[j] = _sort_key(scores(ibuf[...]))

    @pl.when(j == n_steps - 1)
    def _():
        s_new = scores(kin_ref[...])
        t_pos = lax.broadcasted_iota(I32, (t_new, LANES), 0)
        s_pos = lax.broadcasted_iota(I32, (t_new, LANES), 1)
        keyn_ref[...] = jnp.where(s_pos <= t_pos, _sort_key(s_new), NEG_INF_KEY)

        past = n_steps * sc

        def fold(f, init):
            carry = lax.fori_loop(0, n_steps, lambda c, carry: f(carry, keys_ref[c], c * sc), init)
            return f(carry, keyn_ref[...], past)

        def update(f):
            def body(c, carry):
                keys_ref[c] = f(keys_ref[c], c * sc)
                return carry
            lax.fori_loop(0, n_steps, body, 0)
            keyn_ref[...] = f(keyn_ref[...], past)

        thr = _topk_threshold(fold, update, t_new, k_top, (past + LANES).bit_length())
        thr_ref[...] = jnp.broadcast_to(thr, (t_new, LANES))
        m_ref[...] = jnp.full(m_ref.shape, MASK_BIAS, F32)
        l_ref[...] = jnp.zeros(l_ref.shape, F32)
        acc_ref[...] = jnp.zeros(acc_ref.shape, F32)

    def attend(keys, k_of, v_of):
        thr_b = thr_ref[...]
        bias = jnp.concatenate(
            [jnp.where(keys[:, c * LANES:(c + 1) * LANES] >= thr_b, 0.0, MASK_BIAS)
             for c in range(keys.shape[1] // LANES)], axis=-1)
        bias = jnp.concatenate([bias] * N_GROUP, axis=0)
        q = q_ref[...].astype(F32)
        for kv in range(N_KV_HEADS):
            qc = jnp.concatenate(
                [q[:, (kv * N_GROUP + g) * HEAD_DIM:(kv * N_GROUP + g + 1) * HEAD_DIM]
                 for g in range(N_GROUP)], axis=0).astype(BF16)
            _softmax_step(_dot_nt(qc, k_of(kv)) + bias, m_ref, l_ref, acc_ref, v_of(kv), kv)

    @pl.when(j >= n_steps)
    def _():
        for p in range(n_pg):
            for kv in range(N_KV_HEADS):
                rows_kv = pl.ds(kv, page, stride=N_KV_HEADS)
                kbuf[kv, p * page:(p + 1) * page, :] = ck_refs[p][rows_kv, :].astype(BF16)
                vbuf[kv, p * page:(p + 1) * page, :] = cv_refs[p][rows_kv, :].astype(BF16)
        attend(keys_ref[j - n_steps], lambda kv: kbuf[kv], lambda kv: vbuf[kv])

    @pl.when(j == 2 * n_steps - 1)
    def _():
        attend(keyn_ref[...],
               lambda kv: kn_ref[:, kv * HEAD_DIM:(kv + 1) * HEAD_DIM],
               lambda kv: vn_ref[:, kv * HEAD_DIM:(kv + 1) * HEAD_DIM])
        for kv in range(N_KV_HEADS):
            o = acc_ref[kv] / l_ref[kv]
            for g in range(N_GROUP):
                hd = kv * N_GROUP + g
                o_ref[:, hd * HEAD_DIM:(hd + 1) * HEAD_DIM] = (
                    o[g * t_new:(g + 1) * t_new, :].astype(o_ref.dtype))


def _attn_sample(page_table, qi, kw, q, ki2n, kbn, vbn, cache_ik, cache_k, cache_v, n_pg):
    B, T, _ = q.shape
    n_pages = page_table.shape[1]
    page = cache_ik.shape[1]
    assert n_pages % n_pg == 0 and page == LANES and cache_k.shape[1] == page * N_KV_HEADS
    n_steps = n_pages // n_pg
    sc = n_pg * page
    k_top = min(TOPK_MAX, (n_pages * page + T) // 4)
    rows = N_GROUP * T

    per_b = lambda a: pl.BlockSpec((None,) + a.shape[1:], lambda b, j, pt: (b, 0, 0))

    def idx_page(p):
        return pl.BlockSpec((None, page, cache_ik.shape[2]),
                            lambda b, j, pt: (pt[b, jnp.minimum(j, n_steps - 1) * n_pg + p], 0, 0))

    def kv_page(a, p):
        return pl.BlockSpec((None,) + a.shape[1:],
                            lambda b, j, pt: (pt[b, jnp.maximum(j - n_steps, 0) * n_pg + p], 0, 0))

    in_specs = ([per_b(qi), per_b(kw), per_b(q), per_b(ki2n), per_b(kbn), per_b(vbn)]
                + [idx_page(p) for p in range(n_pg)]
                + [kv_page(cache_k, p) for p in range(n_pg)]
                + [kv_page(cache_v, p) for p in range(n_pg)])
    grid_spec = pltpu.PrefetchScalarGridSpec(
        num_scalar_prefetch=1,
        grid=(B, 2 * n_steps),
        in_specs=in_specs,
        out_specs=pl.BlockSpec((None, T, q.shape[2]), lambda b, j, pt: (b, 0, 0)),
        scratch_shapes=[
            pltpu.VMEM((n_steps, T, sc), I32),
            pltpu.VMEM((T, LANES), I32),
            pltpu.VMEM((N_IDX_HEADS * T, LANES), BF16),
            pltpu.VMEM((N_IDX_HEADS * T, LANES), F32),
            pltpu.VMEM((T, LANES), I32),
            pltpu.VMEM((sc, LANES), BF16),
            pltpu.VMEM((N_KV_HEADS, sc, HEAD_DIM), BF16),
            pltpu.VMEM((N_KV_HEADS, sc, HEAD_DIM), BF16),
            pltpu.VMEM((N_KV_HEADS, rows, LANES), F32),
            pltpu.VMEM((N_KV_HEADS, rows, LANES), F32),
            pltpu.VMEM((N_KV_HEADS, rows, HEAD_DIM), F32),
        ],
    )
    return pl.pallas_call(
        functools.partial(_attn_sample_kernel, n_pg=n_pg, n_steps=n_steps, t_new=T, page=page,
                          k_top=k_top, index_scale=(N_IDX_HEADS * IDX_DIM) ** -0.5),
        grid_spec=grid_spec,
        out_shape=jax.ShapeDtypeStruct(q.shape, BF16),
        compiler_params=_params("parallel", "arbitrary"),
        name="attn_sample",
    )(page_table, qi, kw, q, ki2n, kbn, vbn,
      *([cache_ik] * n_pg), *([cache_k] * n_pg), *([cache_v] * n_pg))


def _mix_kernel(x_ref, c_ref, a_ref, w_ref, g1_ref, g2_ref, x1_ref, h_ref, *, c_conv):
    mixed = _dot(c_ref[...], w_ref[0:c_conv, :]) + _dot(a_ref[...], w_ref[c_conv:, :])
    x1 = x_ref[...] + _rms(mixed, g1_ref[...])
    x1_ref[...] = x1
    h_ref[...] = _rms(x1, g2_ref[...]).astype(BF16)


def _mix(x, conv_o, attn_o, w_out, g_post, g_pre, tm):
    T, D = x.shape
    row = lambda n: pl.BlockSpec((tm, n), lambda i: (i, 0))
    return pl.pallas_call(
        functools.partial(_mix_kernel, c_conv=conv_o.shape[1]),
        grid=(T // tm,),
        in_specs=[row(D), row(conv_o.shape[1]), row(attn_o.shape[1]),
                  _resident(w_out.shape), _resident((1, D)), _resident((1, D))],
        out_specs=(row(D), row(D)),
        out_shape=(jax.ShapeDtypeStruct((T, D), F32), jax.ShapeDtypeStruct((T, D), BF16)),
        compiler_params=_params("parallel"),
        name="mix",
    )(x, conv_o, attn_o, w_out, g_post, g_pre)


def _ffn_kernel(x_ref, h_ref, wg_ref, wu_ref, wd_ref, g_ref, y_ref, acc_ref):
    j = pl.program_id(1)
    h = h_ref[...]
    gate = _dot(h, wg_ref[...])
    a = (gate * _sigmoid(gate) * _dot(h, wu_ref[...])).astype(BF16)
    part = _dot(a, wd_ref[...])

    @pl.when(j == 0)
    def _():
        acc_ref[...] = part

    @pl.when(j > 0)
    def _():
        acc_ref[...] += part

    @pl.when(j == pl.num_programs(1) - 1)
    def _():
        y_ref[...] = x_ref[...] + _rms(acc_ref[...], g_ref[...])


def _ffn(x1, h, w_gate, w_up, w_down, g_post, tm, tf):
    T, D = x1.shape
    FF = w_gate.shape[1]
    assert FF % tf == 0
    row = lambda: pl.BlockSpec((tm, D), lambda i, j: (i, 0))
    return pl.pallas_call(
        _ffn_kernel,
        grid=(T // tm, FF // tf),
        in_specs=[row(), row(),
                  pl.BlockSpec((D, tf), lambda i, j: (0, j)),
                  pl.BlockSpec((D, tf), lambda i, j: (0, j)),
                  pl.BlockSpec((tf, D), lambda i, j: (j, 0)),
                  _resident((1, D))],
        out_specs=row(),
        out_shape=jax.ShapeDtypeStruct((T, D), F32),
        scratch_shapes=[pltpu.VMEM((tm, D), F32)],
        compiler_params=_params("parallel", "arbitrary"),
        name="ffn",
    )(x1, h, w_gate, w_up, w_down, g_post)


def _tile(n, pref):
    t = min(pref, n)
    while n % t:
        t //= 2
    return t


def _ffn_tile(ff):
    for t in (512, 256, 128):
        if ff % t == 0:
            return t
    return ff


def _dup_lanes(ki):
    return jnp.concatenate([ki, ki], axis=-1).astype(BF16)


def kernel(x_prompt, x_sample, cache_k, cache_v, cache_idx_k, state_conv, page_table,
           g_pre_mix, w_in, conv_w, conv_b, conv_ln_g, conv_ln_b, w_out,
           g_post_mix, g_pre_ffn, w_gate, w_up, w_down, g_post_ffn):
    B, S, D = x_prompt.shape
    Bd, Td, _ = x_sample.shape
    depth = w_in.shape[0]
    c_conv = conv_w.shape[2]
    c_q = N_HEADS * HEAD_DIM
    c_kv = N_KV_HEADS * HEAD_DIM
    c_qi = N_IDX_HEADS * IDX_DIM
    sizes = (c_conv, c_conv, c_q, c_kv, c_kv, c_qi, IDX_DIM, N_IDX_HEADS)
    n_in = sum(sizes)
    assert w_in.shape[2] == n_in
    n_pad = -n_in % LANES
    n_pool, page = cache_k.shape[1], cache_k.shape[2]

    xp = x_prompt.reshape(B * S, D)
    xs = x_sample.reshape(Bd * Td, D)
    outs = [[] for _ in range(8)]
    row2 = lambda a: a.reshape(1, -1)

    for l in range(depth):
        w_in_b = jnp.pad(w_in[l], ((0, 0), (0, n_pad))).astype(BF16)
        w_out_b = w_out[l].astype(BF16)
        wg_b, wu_b, wd_b = w_gate[l].astype(BF16), w_up[l].astype(BF16), w_down[l].astype(BF16)
        g_pre, g_post, g_ffn_pre, g_ffn_post = (row2(g_pre_mix[l]), row2(g_post_mix[l]),
                                                row2(g_pre_ffn[l]), row2(g_post_ffn[l]))
        cw, cb, lg, lb = conv_w[l], row2(conv_b[l]), row2(conv_ln_g[l]), row2(conv_ln_b[l])
        tf = _ffn_tile(wg_b.shape[1])

        tm = _tile(B * S, 512)
        u, q, k, v, kb, vb, qi, kw = _proj(xp, g_pre, w_in_b, sizes, tm)
        r3 = lambda a: a.reshape(B, S, a.shape[-1])
        conv_o = _conv_prompt(r3(u), cw, cb, lg, lb, _tile(S, 128))
        ki = kw[:, :IDX_DIM]
        attn_o = _attn_prompt(r3(qi), r3(kw), r3(q), r3(_dup_lanes(ki)), r3(kb), r3(vb),
                              qb=_tile(S, 128), sc=_tile(S, 512))
        x1, h2 = _mix(xp, conv_o.reshape(B * S, c_conv), attn_o.reshape(B * S, c_q),
                      w_out_b, g_post, g_ffn_pre, tm)
        xp = _ffn(x1, h2, wg_b, wu_b, wd_b, g_ffn_post, tm, tf)
        outs[0].append(k.reshape(B, S, N_KV_HEADS, HEAD_DIM))
        outs[1].append(v.reshape(B, S, N_KV_HEADS, HEAD_DIM))
        outs[2].append(ki.reshape(B, S, IDX_DIM))
        outs[3].append(r3(u)[:, S - (cw.shape[0] - 1):, :])

        tms = _tile(Bd * Td, 256)
        u, q, k, v, kb, vb, qi, kw = _proj(xs, g_pre, w_in_b, sizes, tms)
        r3 = lambda a: a.reshape(Bd, Td, a.shape[-1])
        conv_o, conv_st = _conv_sample(r3(u), state_conv[l], cw, cb, lg, lb)
        ki = kw[:, :IDX_DIM]
        pad_rows = lambda a: jnp.pad(r3(a), ((0, 0), (0, LANES - Td), (0, 0)))
        attn_o = _attn_sample(
            page_table, r3(qi), r3(kw), r3(q), pad_rows(_dup_lanes(ki)), pad_rows(kb), pad_rows(vb),
            cache_idx_k[l],
            cache_k[l].reshape(n_pool, page * N_KV_HEADS, HEAD_DIM),
            cache_v[l].reshape(n_pool, page * N_KV_HEADS, HEAD_DIM),
            n_pg=_tile(page_table.shape[1], 16))
        x1, h2 = _mix(xs, conv_o.reshape(Bd * Td, c_conv), attn_o.reshape(Bd * Td, c_q),
                      w_out_b, g_post, g_ffn_pre, tms)
        xs = _ffn(x1, h2, wg_b, wu_b, wd_b, g_ffn_post, tms, tf)
        outs[4].append(k.reshape(Bd, Td, N_KV_HEADS, HEAD_DIM))
        outs[5].append(v.reshape(Bd, Td, N_KV_HEADS, HEAD_DIM))
        outs[6].append(ki.reshape(Bd, Td, IDX_DIM))
        outs[7].append(conv_st)

    return (xp.reshape(B, S, D), xs.reshape(Bd, Td, D), *[jnp.stack(o) for o in outs])
```

```python
import functools

import jax
import jax.numpy as jnp
from jax import lax
from jax.experimental import pallas as pl
from jax.experimental.pallas import tpu as pltpu

N_HEADS = 8
N_KV_HEADS = 2
N_GROUP = N_HEADS // N_KV_HEADS
HEAD_DIM = 128
N_IDX_HEADS = 16
IDX_DIM = 64
TOPK_MAX = 256
EPS = 1e-6
LANES = 128
SUBLANES = 8
VMEM_LIMIT = 56 * 1024 * 1024

INT_MIN = -2 ** 31
NEG_INF_KEY = (0xFF800000 ^ 0x7FFFFFFF) - 2 ** 32
MASK_BIAS = -1e30
COUNT_ROWS = 64
LOG2_E = 1.4426950408889634

F32 = jnp.float32
BF16 = jnp.bfloat16
I32 = jnp.int32


def _dot(a, b):
    return jnp.dot(a, b, preferred_element_type=F32)


def _dot_nt(a, b):
    return lax.dot_general(a, b, (((1,), (1,)), ((), ())), preferred_element_type=F32)


def _sigmoid(x):
    return 1.0 / (1.0 + jnp.exp(-x))


def _rms(x, g):
    return x * lax.rsqrt(jnp.mean(x * x, axis=-1, keepdims=True) + EPS) * g


def _sort_key(x):
    b = pltpu.bitcast(x, I32)
    return b ^ ((b >> 31) & 0x7FFFFFFF)


def _params(*sem):
    return pltpu.CompilerParams(dimension_semantics=sem, vmem_limit_bytes=VMEM_LIMIT)


def _resident(shape):
    nd = len(shape)
    return pl.BlockSpec(shape, lambda *_: (0,) * nd, pipeline_mode=pl.Buffered(1))


def _proj_kernel(x_ref, g_ref, w_ref, u_ref, q_ref, k_ref, v_ref, kb_ref, vb_ref,
                 qi_ref, kw_ref, *, offs, attn_scale):
    h = _rms(x_ref[...], g_ref[...]).astype(BF16)
    o_ua, o_ug, o_q, o_k, o_v, o_qi, o_kw, o_end = offs
    ua = _dot(h, w_ref[:, o_ua:o_ug])
    ug = _dot(h, w_ref[:, o_ug:o_q])
    u_ref[...] = ua * _sigmoid(ug)
    q_ref[...] = (_dot(h, w_ref[:, o_q:o_k]) * attn_scale).astype(BF16)
    k = _dot(h, w_ref[:, o_k:o_v])
    k_ref[...] = k
    kb_ref[...] = k.astype(BF16)
    v = _dot(h, w_ref[:, o_v:o_qi])
    v_ref[...] = v
    vb_ref[...] = v.astype(BF16)
    qi_ref[...] = _dot(h, w_ref[:, o_qi:o_kw]).astype(BF16)
    kw_ref[...] = _dot(h, w_ref[:, o_kw:o_end])


def _proj(x, g, w, sizes, tm):
    T, D = x.shape
    c_conv, _, c_q, c_kv, _, c_qi, c_ki, c_wi = sizes
    offs = [0]
    for s in sizes[:6]:
        offs.append(offs[-1] + s)
    offs.append(w.shape[1])
    kw_w = offs[-1] - offs[-2]
    row = lambda n: pl.BlockSpec((tm, n), lambda i: (i, 0))
    out_shape = (
        jax.ShapeDtypeStruct((T, c_conv), F32),
        jax.ShapeDtypeStruct((T, c_q), BF16),
        jax.ShapeDtypeStruct((T, c_kv), F32),
        jax.ShapeDtypeStruct((T, c_kv), F32),
        jax.ShapeDtypeStruct((T, c_kv), BF16),
        jax.ShapeDtypeStruct((T, c_kv), BF16),
        jax.ShapeDtypeStruct((T, c_qi), BF16),
        jax.ShapeDtypeStruct((T, kw_w), F32),
    )
    return pl.pallas_call(
        functools.partial(_proj_kernel, offs=tuple(offs), attn_scale=HEAD_DIM ** -0.5 * LOG2_E),
        grid=(T // tm,),
        in_specs=[row(D), _resident((1, D)), _resident(w.shape)],
        out_specs=tuple(row(s.shape[1]) for s in out_shape),
        out_shape=out_shape,
        compiler_params=_params("parallel"),
        name="proj",
    )(x, g, w)


HIST_PAD = 32


def _conv_body(full_ref, sh_ref, y_ref, w_ref, b_ref, g_ref, beta_ref, o_ref, *, width, tb, cw):
    C = full_ref.shape[1]
    base = HIST_PAD - (width - 1)
    n_b = min(SUBLANES, width)
    n_a = [(width - 1 - b) // SUBLANES + 1 for b in range(n_b)]
    for b in range(n_b):
        rows_b = tb + SUBLANES * (n_a[b] - 1)
        sh_ref[b, 0:rows_b, :] = full_ref[base + b:base + b + rows_b, :]
    for c in range(C // cw):
        cs = slice(c * cw, (c + 1) * cw)
        acc = jnp.zeros((tb, cw), F32)
        for b in range(n_b):
            for a in range(n_a[b]):
                j = b + SUBLANES * a
                acc = acc + sh_ref[b, SUBLANES * a:SUBLANES * a + tb, cs] * w_ref[j:j + 1, cs]
        y_ref[:, cs] = acc + b_ref[:, cs]
    y = y_ref[...]
    mu = jnp.mean(y, axis=-1, keepdims=True)
    yc = y - mu
    z = yc * lax.rsqrt(jnp.mean(yc * yc, axis=-1, keepdims=True) + EPS) * g_ref[...] + beta_ref[...]
    o_ref[...] = (z * _sigmoid(z)).astype(o_ref.dtype)


def _conv_scratch(tb, C, width):
    n_b = min(SUBLANES, width)
    return [pltpu.VMEM((HIST_PAD + tb, C), F32),
            pltpu.VMEM((n_b, tb + SUBLANES * ((width - 1) // SUBLANES), C), F32),
            pltpu.VMEM((tb, C), F32)]


def _conv_prompt_kernel(prev_ref, u_ref, w_ref, b_ref, g_ref, beta_ref, o_ref,
                        full_ref, sh_ref, y_ref, *, width, tb, cw):
    @pl.when(pl.program_id(1) == 0)
    def _():
        full_ref[0:HIST_PAD, :] = jnp.zeros((HIST_PAD, full_ref.shape[1]), F32)

    @pl.when(pl.program_id(1) > 0)
    def _():
        full_ref[0:HIST_PAD, :] = prev_ref[...]

    full_ref[HIST_PAD:HIST_PAD + tb, :] = u_ref[...]
    _conv_body(full_ref, sh_ref, y_ref, w_ref, b_ref, g_ref, beta_ref, o_ref, width=width, tb=tb, cw=cw)


def _conv_prompt(u, conv_w, conv_b, ln_g, ln_b, tb):
    B, S, C = u.shape
    width = conv_w.shape[0]
    r = tb // HIST_PAD
    vec = lambda: _resident((1, C))
    return pl.pallas_call(
        functools.partial(_conv_prompt_kernel, width=width, tb=tb, cw=128),
        grid=(B, S // tb),
        in_specs=[
            pl.BlockSpec((None, HIST_PAD, C), lambda b, i: (b, jnp.maximum(i * r - 1, 0), 0)),
            pl.BlockSpec((None, tb, C), lambda b, i: (b, i, 0)),
            _resident(conv_w.shape), vec(), vec(), vec(),
        ],
        out_specs=pl.BlockSpec((None, tb, C), lambda b, i: (b, i, 0)),
        out_shape=jax.ShapeDtypeStruct((B, S, C), BF16),
        scratch_shapes=_conv_scratch(tb, C, width),
        compiler_params=_params("parallel", "arbitrary"),
        name="conv_prompt",
    )(u, u, conv_w, conv_b, ln_g, ln_b)


def _conv_sample_kernel(hist_ref, u_ref, w_ref, b_ref, g_ref, beta_ref, o_ref, st_ref,
                        full_ref, sh_ref, y_ref, *, width, tb, cw):
    full_ref[0:HIST_PAD, :] = hist_ref[...]
    full_ref[HIST_PAD:HIST_PAD + tb, :] = u_ref[...]
    _conv_body(full_ref, sh_ref, y_ref, w_ref, b_ref, g_ref, beta_ref, o_ref, width=width, tb=tb, cw=cw)
    st_ref[...] = full_ref[HIST_PAD + tb - (width - 1):HIST_PAD + tb, :]


def _conv_sample(u, hist, conv_w, conv_b, ln_g, ln_b):
    B, T, C = u.shape
    width = conv_w.shape[0]
    hist_p = jnp.pad(hist, ((0, 0), (HIST_PAD - (width - 1), 0), (0, 0)))
    vec = lambda: _resident((1, C))
    return pl.pallas_call(
        functools.partial(_conv_sample_kernel, width=width, tb=T, cw=256),
        grid=(B,),
        in_specs=[
            pl.BlockSpec((None, HIST_PAD, C), lambda b: (b, 0, 0)),
            pl.BlockSpec((None, T, C), lambda b: (b, 0, 0)),
            _resident(conv_w.shape), vec(), vec(), vec(),
        ],
        out_specs=(pl.BlockSpec((None, T, C), lambda b: (b, 0, 0)),
                   pl.BlockSpec((None, width - 1, C), lambda b: (b, 0, 0))),
        out_shape=(jax.ShapeDtypeStruct((B, T, C), BF16),
                   jax.ShapeDtypeStruct((B, width - 1, C), F32)),
        scratch_shapes=_conv_scratch(T, C, width),
        compiler_params=_params("parallel"),
        name="conv_sample",
    )(hist_p, u, conv_w, conv_b, ln_g, ln_b)


def _count(fold, rows, pred, row_vals, stage_ref):
    grouped = rows > COUNT_ROWS
    if grouped:
        for n, v in enumerate(row_vals):
            stage_ref[n] = jnp.broadcast_to(v, (rows, LANES))

    def group(cnt, keys, pos0, vals):
        hits = [jnp.where(pred(keys[:, j * LANES:(j + 1) * LANES], pos0 + j * LANES, *vals), 1, 0)
                for j in range(keys.shape[1] // LANES)]
        while len(hits) > 1:
            hits = [a + b for a, b in zip(hits[::2], hits[1::2])] + hits[len(hits) & ~1:]
        return cnt + hits[0]

    def add_block(cnt, keys, pos0):
        if not grouped:
            return group(cnt, keys, pos0, [jnp.broadcast_to(v, (rows, LANES)) for v in row_vals])
        return jnp.concatenate(
            [group(cnt[r:r + COUNT_ROWS], keys[r:r + COUNT_ROWS], pos0,
                   [stage_ref[n, r:r + COUNT_ROWS, :] for n in range(len(row_vals))])
             for r in range(0, rows, COUNT_ROWS)], axis=0)

    cnt = fold(add_block, jnp.zeros((rows, LANES), I32))
    return jnp.sum(cnt.astype(F32), axis=-1, keepdims=True).astype(I32)


def _topk_threshold(fold, update, rows, k_top, n_pos_bits, stage_ref=None):
    def bcast(x):
        return jnp.broadcast_to(x, (rows, LANES))

    def count_ge(cand):
        return _count(fold, rows, lambda keys, pos0, c: keys >= c, [cand], stage_ref)

    def bit_step(it, ub):
        cand_u = ub | jnp.left_shift(jnp.int32(1), 31 - it)
        return jnp.where(count_ge(cand_u ^ INT_MIN) >= k_top, cand_u, ub)

    thr = lax.fori_loop(0, 32, bit_step, jnp.zeros((rows, 1), I32)) ^ INT_MIN
    tie = (count_ge(thr) > k_top) & (thr > NEG_INF_KEY)

    @pl.when(jnp.sum(jnp.where(tie, 1.0, 0.0)) > 0.0)
    def _():
        need = k_top - count_ge(thr + 1)
        thr_b = bcast(thr)
        lane = lax.broadcasted_iota(I32, (rows, LANES), 1)

        def tied_before(cut):
            return _count(fold, rows, lambda keys, pos0, t, c, ln: (keys == t) & (ln + pos0 < c),
                          [thr, cut, lane], stage_ref)

        def cut_step(it, cut):
            cand = cut | jnp.left_shift(jnp.int32(1), n_pos_bits - 1 - it)
            return jnp.where(tied_before(cand) <= need, cand, cut)

        cut = lax.fori_loop(0, n_pos_bits, cut_step, jnp.zeros((rows, 1), I32))
        cut_b = bcast(jnp.where(tie, cut, 2 ** n_pos_bits))

        def drop(keys, pos0):
            parts = []
            for j in range(keys.shape[1] // LANES):
                blk = keys[:, j * LANES:(j + 1) * LANES]
                surplus = (blk == thr_b) & (lane + (pos0 + j * LANES) >= cut_b)
                parts.append(jnp.where(surplus, NEG_INF_KEY, blk))
            return jnp.concatenate(parts, axis=-1)

        update(drop)

    return jnp.maximum(thr, NEG_INF_KEY + 1)


def _softmax_step(s, m_ref, l_ref, acc_ref, v, c):
    m_old = m_ref[c]
    m_new = jnp.maximum(m_old, jnp.max(s, axis=-1, keepdims=True))
    alpha = jnp.exp2(m_old - m_new)
    p = jnp.exp2(s - jnp.tile(m_new, (1, s.shape[1] // LANES)))
    l_ref[c] = alpha * l_ref[c] + jnp.sum(p, axis=-1, keepdims=True)
    acc_ref[c] = alpha * acc_ref[c] + _dot(p.astype(BF16), v)
    m_ref[c] = m_new


def _attn_prompt_kernel(qi_ref, kw_ref, q_ref, ki_ref, k_ref, v_ref, o_ref,
                        keys_ref, qh_ref, wb_ref, stage_ref, m_ref, l_ref, acc_ref,
                        *, qb, sc, k_top, index_scale):
    i = pl.program_id(1)
    n_pos_bits = (keys_ref.shape[0] * sc).bit_length()
    n_chunks = ((i + 1) * qb + sc - 1) // sc

    lane = lax.broadcasted_iota(I32, (qb, LANES), 1)
    for h in range(N_IDX_HEADS):
        blk = qi_ref[:, (h // 2) * LANES:(h // 2 + 1) * LANES]
        keep = (lane < IDX_DIM) if h % 2 == 0 else (lane >= IDX_DIM)
        qh_ref[h] = jnp.where(keep, blk, jnp.zeros_like(blk))
        w_h = kw_ref[:, IDX_DIM + h:IDX_DIM + h + 1] * index_scale
        wb_ref[h] = jnp.broadcast_to(w_h, (qb, LANES))

    q_pos = i * qb + lax.broadcasted_iota(I32, (qb, sc), 0)

    def score_chunk(c, carry):
        ks = ki_ref[pl.ds(pl.multiple_of(c * sc, sc), sc), :]
        parts = [jnp.zeros((qb, LANES), F32) for _ in range(sc // LANES)]
        for h in range(N_IDX_HEADS):
            logit = _dot_nt(qh_ref[h], ks)
            w_b = wb_ref[h]
            for j in range(sc // LANES):
                parts[j] = parts[j] + jnp.maximum(logit[:, j * LANES:(j + 1) * LANES], 0.0) * w_b
        score = jnp.concatenate(parts, axis=-1)
        s_pos = c * sc + lax.broadcasted_iota(I32, (qb, sc), 1)
        keys_ref[c] = jnp.where(s_pos <= q_pos, _sort_key(score), NEG_INF_KEY)
        return carry

    lax.fori_loop(0, n_chunks, score_chunk, 0)

    def fold(f, init):
        return lax.fori_loop(0, n_chunks, lambda c, carry: f(carry, keys_ref[c], c * sc), init)

    def update(f):
        def body(c, carry):
            keys_ref[c] = f(keys_ref[c], c * sc)
            return carry
        lax.fori_loop(0, n_chunks, body, 0)

    thr = _topk_threshold(fold, update, qb, k_top, n_pos_bits, stage_ref)
    thr_b = jnp.broadcast_to(thr, (qb, LANES))

    rows = N_GROUP * qb
    m_ref[...] = jnp.full(m_ref.shape, MASK_BIAS, F32)
    l_ref[...] = jnp.zeros(l_ref.shape, F32)
    acc_ref[...] = jnp.zeros(acc_ref.shape, F32)

    def attn_chunk(c, carry):
        keys = keys_ref[c]
        bias = jnp.concatenate(
            [jnp.where(keys[:, j * LANES:(j + 1) * LANES] >= thr_b, 0.0, MASK_BIAS)
             for j in range(sc // LANES)], axis=-1)
        bias = jnp.concatenate([bias] * N_GROUP, axis=0)
        start = pl.multiple_of(c * sc, sc)
        for kv in range(N_KV_HEADS):
            qc = jnp.concatenate(
                [q_ref[:, (kv * N_GROUP + g) * HEAD_DIM:(kv * N_GROUP + g + 1) * HEAD_DIM]
                 for g in range(N_GROUP)], axis=0)
            kc = k_ref[pl.ds(start, sc), kv * HEAD_DIM:(kv + 1) * HEAD_DIM]
            vc = v_ref[pl.ds(start, sc), kv * HEAD_DIM:(kv + 1) * HEAD_DIM]
            _softmax_step(_dot_nt(qc, kc) + bias, m_ref, l_ref, acc_ref, vc, kv)
        return carry

    lax.fori_loop(0, n_chunks, attn_chunk, 0)

    for kv in range(N_KV_HEADS):
        o = acc_ref[kv] / l_ref[kv]
        for g in range(N_GROUP):
            hd = kv * N_GROUP + g
            o_ref[:, hd * HEAD_DIM:(hd + 1) * HEAD_DIM] = o[g * qb:(g + 1) * qb, :].astype(o_ref.dtype)


def _attn_prompt(qi, kw, q, ki2, kb, vb, qb, sc):
    B, S, _ = q.shape
    k_top = min(TOPK_MAX, S // 4)
    assert sc >= k_top and S % sc == 0 and S % qb == 0 and sc % qb == 0
    qblk = lambda n: pl.BlockSpec((None, qb, n), lambda b, i: (b, i, 0))
    whole = lambda n: pl.BlockSpec((None, S, n), lambda b, i: (b, 0, 0))
    rows = N_GROUP * qb
    return pl.pallas_call(
        functools.partial(_attn_prompt_kernel, qb=qb, sc=sc, k_top=k_top,
                          index_scale=(N_IDX_HEADS * IDX_DIM) ** -0.5),
        grid=(B, S // qb),
        in_specs=[qblk(qi.shape[2]), qblk(kw.shape[2]), qblk(q.shape[2]),
                  whole(ki2.shape[2]), whole(kb.shape[2]), whole(vb.shape[2])],
        out_specs=qblk(q.shape[2]),
        out_shape=jax.ShapeDtypeStruct(q.shape, BF16),
        scratch_shapes=[
            pltpu.VMEM((S // sc, qb, sc), I32),
            pltpu.VMEM((N_IDX_HEADS, qb, LANES), BF16),
            pltpu.VMEM((N_IDX_HEADS, qb, LANES), F32),
            pltpu.VMEM((3, qb, LANES), I32),
            pltpu.VMEM((N_KV_HEADS, rows, LANES), F32),
            pltpu.VMEM((N_KV_HEADS, rows, LANES), F32),
            pltpu.VMEM((N_KV_HEADS, rows, HEAD_DIM), F32),
        ],
        compiler_params=_params("parallel", "arbitrary"),
        name="attn_prompt",
    )(qi, kw, q, ki2, kb, vb)


def _attn_sample_kernel(pt_ref, qi_ref, kw_ref, q_ref, kin_ref, kn_ref, vn_ref,
                        ci_hbm, ck_hbm, cv_hbm, o_ref,
                        keys_ref, keyn_ref, ibuf, kbuf, vbuf, m_ref, l_ref, acc_ref,
                        ipg, kpg, vpg, sem_i, sem_k, sem_v,
                        *, n_pg, n_steps, t_new, page, k_top, index_scale):
    b = pl.program_id(0)
    sc = n_pg * page

    def group_copies(src, dst, sem, step, slot):
        return [pltpu.make_async_copy(src.at[pt_ref[b, step * n_pg + p]], dst.at[slot, p], sem.at[slot])
                for p in range(n_pg)]

    def start_group(src, dst, sem, step, slot):
        for cp in group_copies(src, dst, sem, step, slot):
            cp.start()

    def wait_group(src, dst, sem, step, slot):
        for cp in group_copies(src, dst, sem, step, slot):
            cp.wait()

    start_group(ci_hbm, ipg, sem_i, 0, 0)
    start_group(ck_hbm, kpg, sem_k, 0, 0)
    start_group(cv_hbm, vpg, sem_v, 0, 0)

    def head_sum(x):
        return jnp.sum(x.reshape(N_IDX_HEADS, t_new, x.shape[-1]), axis=0)

    qi = qi_ref[...].astype(F32)
    lane = lax.broadcasted_iota(I32, (t_new, LANES), 1)
    pieces, wcols = [], []
    for h in range(N_IDX_HEADS):
        blk = qi[:, (h // 2) * LANES:(h // 2 + 1) * LANES]
        keep = (lane < IDX_DIM) if h % 2 == 0 else (lane >= IDX_DIM)
        pieces.append(jnp.where(keep, blk, 0.0))
        w_h = kw_ref[:, IDX_DIM + h:IDX_DIM + h + 1] * index_scale
        wcols.append(jnp.broadcast_to(w_h, (t_new, LANES)))
    lhs = jnp.concatenate(pieces, axis=0).astype(BF16)
    w_b = jnp.concatenate(wcols, axis=0)

    def scores(key_mat):
        logit = _dot(lhs, key_mat)
        parts = [head_sum(jnp.maximum(logit[:, c * LANES:(c + 1) * LANES], 0.0) * w_b)
                 for c in range(key_mat.shape[1] // LANES)]
        return jnp.concatenate(parts, axis=-1)

    def score_step(s, carry):
        slot = s % 2

        @pl.when(s + 1 < n_steps)
        def _():
            start_group(ci_hbm, ipg, sem_i, s + 1, 1 - slot)

        wait_group(ci_hbm, ipg, sem_i, s, slot)
        for p in range(n_pg):
            pg = ipg[slot, p].astype(BF16)
            ibuf[:, p * page:(p + 1) * page] = jnp.concatenate([pg, pg], axis=0)
        keys_ref[s] = _sort_key(scores(ibuf[...]))
        return carry

    lax.fori_loop(0, n_steps, score_step, 0)

    s_new = scores(kin_ref[...])
    t_pos = lax.broadcasted_iota(I32, (t_new, LANES), 0)
    s_pos = lax.broadcasted_iota(I32, (t_new, LANES), 1)
    keyn_ref[...] = jnp.where(s_pos <= t_pos, _sort_key(s_new), NEG_INF_KEY)

    past = n_steps * sc

    def fold(f, init):
        carry = init
        for c in range(n_steps):
            carry = f(carry, keys_ref[c], c * sc)
        return f(carry, keyn_ref[...], past)

    def update(f):
        for c in range(n_steps):
            keys_ref[c] = f(keys_ref[c], c * sc)
        keyn_ref[...] = f(keyn_ref[...], past)

    thr = _topk_threshold(fold, update, t_new, k_top, (past + LANES).bit_length())
    thr_b = jnp.broadcast_to(thr, (t_new, LANES))
    m_ref[...] = jnp.full(m_ref.shape, MASK_BIAS, F32)
    l_ref[...] = jnp.zeros(l_ref.shape, F32)
    acc_ref[...] = jnp.zeros(acc_ref.shape, F32)

    def attend(keys, k_of, v_of):
        bias = jnp.concatenate(
            [jnp.where(keys[:, c * LANES:(c + 1) * LANES] >= thr_b, 0.0, MASK_BIAS)
             for c in range(keys.shape[1] // LANES)], axis=-1)
        bias = jnp.concatenate([bias] * N_GROUP, axis=0)
        q = q_ref[...].astype(F32)
        for kv in range(N_KV_HEADS):
            qc = jnp.concatenate(
                [q[:, (kv * N_GROUP + g) * HEAD_DIM:(kv * N_GROUP + g + 1) * HEAD_DIM]
                 for g in range(N_GROUP)], axis=0).astype(BF16)
            _softmax_step(_dot_nt(qc, k_of(kv)) + bias, m_ref, l_ref, acc_ref, v_of(kv), kv)

    def attend_step(s, carry):
        slot = s % 2

        @pl.when(s + 1 < n_steps)
        def _():
            start_group(ck_hbm, kpg, sem_k, s + 1, 1 - slot)
            start_group(cv_hbm, vpg, sem_v, s + 1, 1 - slot)

        wait_group(ck_hbm, kpg, sem_k, s, slot)
        wait_group(cv_hbm, vpg, sem_v, s, slot)
        for p in range(n_pg):
            for kv in range(N_KV_HEADS):
                rows_kv = pl.ds(kv, page, stride=N_KV_HEADS)
                kbuf[kv, p * page:(p + 1) * page, :] = kpg[slot, p, rows_kv, :].astype(BF16)
                vbuf[kv, p * page:(p + 1) * page, :] = vpg[slot, p, rows_kv, :].astype(BF16)
        attend(keys_ref[s], lambda kv: kbuf[kv], lambda kv: vbuf[kv])
        return carry

    lax.fori_loop(0, n_steps, attend_step, 0)

    attend(keyn_ref[...],
           lambda kv: kn_ref[:, kv * HEAD_DIM:(kv + 1) * HEAD_DIM],
           lambda kv: vn_ref[:, kv * HEAD_DIM:(kv + 1) * HEAD_DIM])
    for kv in range(N_KV_HEADS):
        o = acc_ref[kv] / l_ref[kv]
        for g in range(N_GROUP):
            hd = kv * N_GROUP + g
            o_ref[:, hd * HEAD_DIM:(hd + 1) * HEAD_DIM] = (
                o[g * t_new:(g + 1) * t_new, :].astype(o_ref.dtype))


def _attn_sample(page_table, qi, kw, q, ki2n_t, kbn, vbn, cache_ik_t, cache_k, cache_v, n_pg):
    B, T, _ = q.shape
    n_pages = page_table.shape[1]
    page = cache_ik_t.shape[2]
    assert n_pages % n_pg == 0 and page == LANES and cache_k.shape[1] == page * N_KV_HEADS
    assert cache_ik_t.shape[1] == IDX_DIM
    n_steps = n_pages // n_pg
    sc = n_pg * page
    k_top = min(TOPK_MAX, (n_pages * page + T) // 4)
    rows = N_GROUP * T

    per_b = lambda a: pl.BlockSpec((None,) + a.shape[1:], lambda b, pt: (b, 0, 0))
    hbm = pl.BlockSpec(memory_space=pl.ANY)
    grid_spec = pltpu.PrefetchScalarGridSpec(
        num_scalar_prefetch=1,
        grid=(B,),
        in_specs=[per_b(qi), per_b(kw), per_b(q), per_b(ki2n_t), per_b(kbn), per_b(vbn), hbm, hbm, hbm],
        out_specs=pl.BlockSpec((None, T, q.shape[2]), lambda b, pt: (b, 0, 0)),
        scratch_shapes=[
            pltpu.VMEM((n_steps, T, sc), I32),
            pltpu.VMEM((T, LANES), I32),
            pltpu.VMEM((2 * IDX_DIM, sc), BF16),
            pltpu.VMEM((N_KV_HEADS, sc, HEAD_DIM), BF16),
            pltpu.VMEM((N_KV_HEADS, sc, HEAD_DIM), BF16),
            pltpu.VMEM((N_KV_HEADS, rows, LANES), F32),
            pltpu.VMEM((N_KV_HEADS, rows, LANES), F32),
            pltpu.VMEM((N_KV_HEADS, rows, HEAD_DIM), F32),
            pltpu.VMEM((2, n_pg) + cache_ik_t.shape[1:], cache_ik_t.dtype),
            pltpu.VMEM((2, n_pg) + cache_k.shape[1:], cache_k.dtype),
            pltpu.VMEM((2, n_pg) + cache_v.shape[1:], cache_v.dtype),
            pltpu.SemaphoreType.DMA((2,)),
            pltpu.SemaphoreType.DMA((2,)),
            pltpu.SemaphoreType.DMA((2,)),
        ],
    )
    return pl.pallas_call(
        functools.partial(_attn_sample_kernel, n_pg=n_pg, n_steps=n_steps, t_new=T, page=page,
                          k_top=k_top, index_scale=(N_IDX_HEADS * IDX_DIM) ** -0.5),
        grid_spec=grid_spec,
        out_shape=jax.ShapeDtypeStruct(q.shape, BF16),
        compiler_params=_params("arbitrary"),
        name="attn_sample",
    )(page_table, qi, kw, q, ki2n_t, kbn, vbn, cache_ik_t, cache_k, cache_v)


def _mix_kernel(x_ref, c_ref, a_ref, w_ref, g1_ref, g2_ref, x1_ref, h_ref, *, c_conv):
    mixed = _dot(c_ref[...], w_ref[0:c_conv, :]) + _dot(a_ref[...], w_ref[c_conv:, :])
    x1 = x_ref[...] + _rms(mixed, g1_ref[...])
    x1_ref[...] = x1
    h_ref[...] = _rms(x1, g2_ref[...]).astype(BF16)


def _mix(x, conv_o, attn_o, w_out, g_post, g_pre, tm):
    T, D = x.shape
    row = lambda n: pl.BlockSpec((tm, n), lambda i: (i, 0))
    return pl.pallas_call(
        functools.partial(_mix_kernel, c_conv=conv_o.shape[1]),
        grid=(T // tm,),
        in_specs=[row(D), row(conv_o.shape[1]), row(attn_o.shape[1]),
                  _resident(w_out.shape), _resident((1, D)), _resident((1, D))],
        out_specs=(row(D), row(D)),
        out_shape=(jax.ShapeDtypeStruct((T, D), F32), jax.ShapeDtypeStruct((T, D), BF16)),
        compiler_params=_params("parallel"),
        name="mix",
    )(x, conv_o, attn_o, w_out, g_post, g_pre)


def _ffn_kernel(x_ref, h_ref, wg_ref, wu_ref, wd_ref, g_ref, y_ref, acc_ref):
    j = pl.program_id(1)
    h = h_ref[...]
    gate = _dot(h, wg_ref[...])
    a = (gate * _sigmoid(gate) * _dot(h, wu_ref[...])).astype(BF16)
    part = _dot(a, wd_ref[...])

    @pl.when(j == 0)
    def _():
        acc_ref[...] = part

    @pl.when(j > 0)
    def _():
        acc_ref[...] += part

    @pl.when(j == pl.num_programs(1) - 1)
    def _():
        y_ref[...] = x_ref[...] + _rms(acc_ref[...], g_ref[...])


def _ffn(x1, h, w_gate, w_up, w_down, g_post, tm, tf):
    T, D = x1.shape
    FF = w_gate.shape[1]
    assert FF % tf == 0
    row = lambda: pl.BlockSpec((tm, D), lambda i, j: (i, 0))
    return pl.pallas_call(
        _ffn_kernel,
        grid=(T // tm, FF // tf),
        in_specs=[row(), row(),
                  pl.BlockSpec((D, tf), lambda i, j: (0, j)),
                  pl.BlockSpec((D, tf), lambda i, j: (0, j)),
                  pl.BlockSpec((tf, D), lambda i, j: (j, 0)),
                  _resident((1, D))],
        out_specs=row(),
        out_shape=jax.ShapeDtypeStruct((T, D), F32),
        scratch_shapes=[pltpu.VMEM((tm, D), F32)],
        compiler_params=_params("parallel", "arbitrary"),
        name="ffn",
    )(x1, h, w_gate, w_up, w_down, g_post)


def _tile(n, pref):
    t = min(pref, n)
    while n % t:
        t //= 2
    return t


def _ffn_tile(ff):
    for t in (512, 256, 128):
        if ff % t == 0:
            return t
    return ff


def _dup_lanes(ki):
    return jnp.concatenate([ki, ki], axis=-1).astype(BF16)


def kernel(x_prompt, x_sample, cache_k, cache_v, cache_idx_k, state_conv, page_table,
           g_pre_mix, w_in, conv_w, conv_b, conv_ln_g, conv_ln_b, w_out,
           g_post_mix, g_pre_ffn, w_gate, w_up, w_down, g_post_ffn):
    B, S, D = x_prompt.shape
    Bd, Td, _ = x_sample.shape
    depth = w_in.shape[0]
    c_conv = conv_w.shape[2]
    c_q = N_HEADS * HEAD_DIM
    c_kv = N_KV_HEADS * HEAD_DIM
    c_qi = N_IDX_HEADS * IDX_DIM
    sizes = (c_conv, c_conv, c_q, c_kv, c_kv, c_qi, IDX_DIM, N_IDX_HEADS)
    n_in = sum(sizes)
    assert w_in.shape[2] == n_in
    n_pad = -n_in % LANES
    n_pool, page = cache_k.shape[1], cache_k.shape[2]

    xp = x_prompt.reshape(B * S, D)
    xs = x_sample.reshape(Bd * Td, D)
    outs = [[] for _ in range(8)]
    row2 = lambda a: a.reshape(1, -1)

    for l in range(depth):
        w_in_b = jnp.pad(w_in[l], ((0, 0), (0, n_pad))).astype(BF16)
        w_out_b = w_out[l].astype(BF16)
        wg_b, wu_b, wd_b = w_gate[l].astype(BF16), w_up[l].astype(BF16), w_down[l].astype(BF16)
        g_pre, g_post, g_ffn_pre, g_ffn_post = (row2(g_pre_mix[l]), row2(g_post_mix[l]),
                                                row2(g_pre_ffn[l]), row2(g_post_ffn[l]))
        cw, cb, lg, lb = conv_w[l], row2(conv_b[l]), row2(conv_ln_g[l]), row2(conv_ln_b[l])
        tf = _ffn_tile(wg_b.shape[1])

        tm = _tile(B * S, 512)
        u, q, k, v, kb, vb, qi, kw = _proj(xp, g_pre, w_in_b, sizes, tm)
        r3 = lambda a: a.reshape(B, S, a.shape[-1])
        conv_o = _conv_prompt(r3(u), cw, cb, lg, lb, _tile(S, 128))
        ki = kw[:, :IDX_DIM]
        attn_o = _attn_prompt(r3(qi), r3(kw), r3(q), r3(_dup_lanes(ki)), r3(kb), r3(vb),
                              qb=_tile(S, 256), sc=_tile(S, 512))
        x1, h2 = _mix(xp, conv_o.reshape(B * S, c_conv), attn_o.reshape(B * S, c_q),
                      w_out_b, g_post, g_ffn_pre, tm)
        xp = _ffn(x1, h2, wg_b, wu_b, wd_b, g_ffn_post, tm, tf)
        outs[0].append(k.reshape(B, S, N_KV_HEADS, HEAD_DIM))
        outs[1].append(v.reshape(B, S, N_KV_HEADS, HEAD_DIM))
        outs[2].append(ki.reshape(B, S, IDX_DIM))
        outs[3].append(r3(u)[:, S - (cw.shape[0] - 1):, :])

        tms = _tile(Bd * Td, 256)
        u, q, k, v, kb, vb, qi, kw = _proj(xs, g_pre, w_in_b, sizes, tms)
        r3 = lambda a: a.reshape(Bd, Td, a.shape[-1])
        conv_o, conv_st = _conv_sample(r3(u), state_conv[l], cw, cb, lg, lb)
        ki = kw[:, :IDX_DIM]
        pad_rows = lambda a: jnp.pad(r3(a), ((0, 0), (0, LANES - Td), (0, 0)))
        ki_t = jnp.pad(jnp.swapaxes(r3(ki), 1, 2), ((0, 0), (0, 0), (0, LANES - Td))).astype(BF16)
        attn_o = _attn_sample(
            page_table, r3(qi), r3(kw), r3(q), jnp.concatenate([ki_t, ki_t], axis=1),
            pad_rows(kb), pad_rows(vb),
            jnp.swapaxes(cache_idx_k[l], 1, 2),

            cache_k[l].reshape(n_pool, page * N_KV_HEADS, HEAD_DIM),
            cache_v[l].reshape(n_pool, page * N_KV_HEADS, HEAD_DIM),
            n_pg=_tile(page_table.shape[1], 16))
        x1, h2 = _mix(xs, conv_o.reshape(Bd * Td, c_conv), attn_o.reshape(Bd * Td, c_q),
                      w_out_b, g_post, g_ffn_pre, tms)
        xs = _ffn(x1, h2, wg_b, wu_b, wd_b, g_ffn_post, tms, tf)
        outs[4].append(k.reshape(Bd, Td, N_KV_HEADS, HEAD_DIM))
        outs[5].append(v.reshape(Bd, Td, N_KV_HEADS, HEAD_DIM))
        outs[6].append(ki.reshape(Bd, Td, IDX_DIM))
        outs[7].append(conv_st)

    return (xp.reshape(B, S, D), xs.reshape(Bd, Td, D), *[jnp.stack(o) for o in outs])
```

```python
import functools

import jax
import jax.numpy as jnp
from jax import lax
from jax.experimental import pallas as pl
from jax.experimental.pallas import tpu as pltpu

N_HEADS = 8
N_KV_HEADS = 2
N_GROUP = N_HEADS // N_KV_HEADS
HEAD_DIM = 128
N_IDX_HEADS = 16
IDX_DIM = 64
TOPK_MAX = 256
EPS = 1e-6
LANES = 128
SUBLANES = 8
VMEM_LIMIT = 56 * 1024 * 1024

INT_MIN = -2 ** 31
NEG_INF_KEY = (0xFF800000 ^ 0x7FFFFFFF) - 2 ** 32
MASK_BIAS = -1e30
COUNT_ROWS = 64
KV_SLOTS = 4
LOG2_E = 1.4426950408889634

F32 = jnp.float32
BF16 = jnp.bfloat16
I32 = jnp.int32


def _dot(a, b):
    return jnp.dot(a, b, preferred_element_type=F32)


def _dot_nt(a, b):
    return lax.dot_general(a, b, (((1,), (1,)), ((), ())), preferred_element_type=F32)


def _sigmoid(x):
    return 1.0 / (1.0 + jnp.exp(-x))


def _rms(x, g):
    return x * lax.rsqrt(jnp.mean(x * x, axis=-1, keepdims=True) + EPS) * g


def _sort_key(x):
    b = pltpu.bitcast(x, I32)
    return b ^ ((b >> 31) & 0x7FFFFFFF)


def _params(*sem):
    return pltpu.CompilerParams(dimension_semantics=sem, vmem_limit_bytes=VMEM_LIMIT)


def _resident(shape):
    nd = len(shape)
    return pl.BlockSpec(shape, lambda *_: (0,) * nd, pipeline_mode=pl.Buffered(1))


def _proj_kernel(x_ref, g_ref, w_ref, u_ref, q_ref, k_ref, v_ref, kb_ref, vb_ref,
                 qi_ref, kw_ref, *, offs, attn_scale):
    h = _rms(x_ref[...], g_ref[...]).astype(BF16)
    o_ua, o_ug, o_q, o_k, o_v, o_qi, o_kw, o_end = offs
    ua = _dot(h, w_ref[:, o_ua:o_ug])
    ug = _dot(h, w_ref[:, o_ug:o_q])
    u_ref[...] = ua * _sigmoid(ug)
    q_ref[...] = (_dot(h, w_ref[:, o_q:o_k]) * attn_scale).astype(BF16)
    k = _dot(h, w_ref[:, o_k:o_v])
    k_ref[...] = k
    kb_ref[...] = k.astype(BF16)
    v = _dot(h, w_ref[:, o_v:o_qi])
    v_ref[...] = v
    vb_ref[...] = v.astype(BF16)
    qi_ref[...] = _dot(h, w_ref[:, o_qi:o_kw]).astype(BF16)
    kw_ref[...] = _dot(h, w_ref[:, o_kw:o_end])


def _proj(x, g, w, sizes, tm):
    T, D = x.shape
    c_conv, _, c_q, c_kv, _, c_qi, c_ki, c_wi = sizes
    offs = [0]
    for s in sizes[:6]:
        offs.append(offs[-1] + s)
    offs.append(w.shape[1])
    kw_w = offs[-1] - offs[-2]
    row = lambda n: pl.BlockSpec((tm, n), lambda i: (i, 0))
    out_shape = (
        jax.ShapeDtypeStruct((T, c_conv), F32),
        jax.ShapeDtypeStruct((T, c_q), BF16),
        jax.ShapeDtypeStruct((T, c_kv), F32),
        jax.ShapeDtypeStruct((T, c_kv), F32),
        jax.ShapeDtypeStruct((T, c_kv), BF16),
        jax.ShapeDtypeStruct((T, c_kv), BF16),
        jax.ShapeDtypeStruct((T, c_qi), BF16),
        jax.ShapeDtypeStruct((T, kw_w), F32),
    )
    return pl.pallas_call(
        functools.partial(_proj_kernel, offs=tuple(offs), attn_scale=HEAD_DIM ** -0.5 * LOG2_E),
        grid=(T // tm,),
        in_specs=[row(D), _resident((1, D)), _resident(w.shape)],
        out_specs=tuple(row(s.shape[1]) for s in out_shape),
        out_shape=out_shape,
        compiler_params=_params("parallel"),
        name="proj",
    )(x, g, w)


HIST_PAD = 32


def _conv_body(full_ref, sh_ref, y_ref, w_ref, b_ref, g_ref, beta_ref, o_ref, *, width, tb, cw):
    C = full_ref.shape[1]
    base = HIST_PAD - (width - 1)
    n_b = min(SUBLANES, width)
    n_a = [(width - 1 - b) // SUBLANES + 1 for b in range(n_b)]
    for b in range(n_b):
        rows_b = tb + SUBLANES * (n_a[b] - 1)
        sh_ref[b, 0:rows_b, :] = full_ref[base + b:base + b + rows_b, :]
    for c in range(C // cw):
        cs = slice(c * cw, (c + 1) * cw)
        acc = jnp.zeros((tb, cw), F32)
        for b in range(n_b):
            for a in range(n_a[b]):
                j = b + SUBLANES * a
                acc = acc + sh_ref[b, SUBLANES * a:SUBLANES * a + tb, cs] * w_ref[j:j + 1, cs]
        y_ref[:, cs] = acc + b_ref[:, cs]
    y = y_ref[...]
    mu = jnp.mean(y, axis=-1, keepdims=True)
    yc = y - mu
    z = yc * lax.rsqrt(jnp.mean(yc * yc, axis=-1, keepdims=True) + EPS) * g_ref[...] + beta_ref[...]
    o_ref[...] = (z * _sigmoid(z)).astype(o_ref.dtype)


def _conv_scratch(tb, C, width):
    n_b = min(SUBLANES, width)
    return [pltpu.VMEM((HIST_PAD + tb, C), F32),
            pltpu.VMEM((n_b, tb + SUBLANES * ((width - 1) // SUBLANES), C), F32),
            pltpu.VMEM((tb, C), F32)]


def _conv_prompt_kernel(prev_ref, u_ref, w_ref, b_ref, g_ref, beta_ref, o_ref,
                        full_ref, sh_ref, y_ref, *, width, tb, cw):
    @pl.when(pl.program_id(1) == 0)
    def _():
        full_ref[0:HIST_PAD, :] = jnp.zeros((HIST_PAD, full_ref.shape[1]), F32)

    @pl.when(pl.program_id(1) > 0)
    def _():
        full_ref[0:HIST_PAD, :] = prev_ref[...]

    full_ref[HIST_PAD:HIST_PAD + tb, :] = u_ref[...]
    _conv_body(full_ref, sh_ref, y_ref, w_ref, b_ref, g_ref, beta_ref, o_ref, width=width, tb=tb, cw=cw)


def _conv_prompt(u, conv_w, conv_b, ln_g, ln_b, tb):
    B, S, C = u.shape
    width = conv_w.shape[0]
    r = tb // HIST_PAD
    vec = lambda: _resident((1, C))
    return pl.pallas_call(
        functools.partial(_conv_prompt_kernel, width=width, tb=tb, cw=128),
        grid=(B, S // tb),
        in_specs=[
            pl.BlockSpec((None, HIST_PAD, C), lambda b, i: (b, jnp.maximum(i * r - 1, 0), 0)),
            pl.BlockSpec((None, tb, C), lambda b, i: (b, i, 0)),
            _resident(conv_w.shape), vec(), vec(), vec(),
        ],
        out_specs=pl.BlockSpec((None, tb, C), lambda b, i: (b, i, 0)),
        out_shape=jax.ShapeDtypeStruct((B, S, C), BF16),
        scratch_shapes=_conv_scratch(tb, C, width),
        compiler_params=_params("parallel", "arbitrary"),
        name="conv_prompt",
    )(u, u, conv_w, conv_b, ln_g, ln_b)


def _conv_sample_kernel(hist_ref, u_ref, w_ref, b_ref, g_ref, beta_ref, o_ref, st_ref,
                        full_ref, sh_ref, y_ref, *, width, tb, cw):
    full_ref[0:HIST_PAD, :] = hist_ref[...]
    full_ref[HIST_PAD:HIST_PAD + tb, :] = u_ref[...]
    _conv_body(full_ref, sh_ref, y_ref, w_ref, b_ref, g_ref, beta_ref, o_ref, width=width, tb=tb, cw=cw)
    st_ref[...] = full_ref[HIST_PAD + tb - (width - 1):HIST_PAD + tb, :]


def _conv_sample(u, hist, conv_w, conv_b, ln_g, ln_b):
    B, T, C = u.shape
    width = conv_w.shape[0]
    hist_p = jnp.pad(hist, ((0, 0), (HIST_PAD - (width - 1), 0), (0, 0)))
    vec = lambda: _resident((1, C))
    return pl.pallas_call(
        functools.partial(_conv_sample_kernel, width=width, tb=T, cw=256),
        grid=(B,),
        in_specs=[
            pl.BlockSpec((None, HIST_PAD, C), lambda b: (b, 0, 0)),
            pl.BlockSpec((None, T, C), lambda b: (b, 0, 0)),
            _resident(conv_w.shape), vec(), vec(), vec(),
        ],
        out_specs=(pl.BlockSpec((None, T, C), lambda b: (b, 0, 0)),
                   pl.BlockSpec((None, width - 1, C), lambda b: (b, 0, 0))),
        out_shape=(jax.ShapeDtypeStruct((B, T, C), BF16),
                   jax.ShapeDtypeStruct((B, width - 1, C), F32)),
        scratch_shapes=_conv_scratch(T, C, width),
        compiler_params=_params("parallel"),
        name="conv_sample",
    )(hist_p, u, conv_w, conv_b, ln_g, ln_b)


def _count(fold, rows, pred, row_vals, stage_ref):
    grouped = rows > COUNT_ROWS
    if grouped:
        for n, v in enumerate(row_vals):
            stage_ref[n] = jnp.broadcast_to(v, (rows, LANES))

    def group(cnt, keys, pos0, vals):
        hits = [jnp.where(pred(keys[:, j * LANES:(j + 1) * LANES], pos0 + j * LANES, *vals), 1, 0)
                for j in range(keys.shape[1] // LANES)]
        while len(hits) > 1:
            hits = [a + b for a, b in zip(hits[::2], hits[1::2])] + hits[len(hits) & ~1:]
        return cnt + hits[0]

    def add_block(cnt, keys, pos0):
        if not grouped:
            return group(cnt, keys, pos0, [jnp.broadcast_to(v, (rows, LANES)) for v in row_vals])
        return jnp.concatenate(
            [group(cnt[r:r + COUNT_ROWS], keys[r:r + COUNT_ROWS], pos0,
                   [stage_ref[n, r:r + COUNT_ROWS, :] for n in range(len(row_vals))])
             for r in range(0, rows, COUNT_ROWS)], axis=0)

    cnt = fold(add_block, jnp.zeros((rows, LANES), I32))
    return jnp.sum(cnt.astype(F32), axis=-1, keepdims=True).astype(I32)


def _topk_threshold(fold, update, rows, k_top, n_pos_bits, stage_ref=None):
    def bcast(x):
        return jnp.broadcast_to(x, (rows, LANES))

    def count_ge(cand):
        return _count(fold, rows, lambda keys, pos0, c: keys >= c, [cand], stage_ref)

    def bit_step(it, state):
        ub, c_ub = state
        cand_u = ub | jnp.left_shift(jnp.int32(1), 31 - it)
        cnt = count_ge(cand_u ^ INT_MIN)
        keep = cnt >= k_top
        return jnp.where(keep, cand_u, ub), jnp.where(keep, cnt, c_ub)

    start = (jnp.zeros((rows, 1), I32), jnp.full((rows, 1), k_top + 1, I32))
    ub, c_thr = lax.fori_loop(0, 32, bit_step, start)
    thr = ub ^ INT_MIN
    tie = (c_thr > k_top) & (thr > NEG_INF_KEY)

    @pl.when(jnp.sum(jnp.where(tie, 1.0, 0.0)) > 0.0)
    def _():
        need = k_top - count_ge(thr + 1)
        thr_b = bcast(thr)
        lane = lax.broadcasted_iota(I32, (rows, LANES), 1)

        def tied_before(cut):
            return _count(fold, rows, lambda keys, pos0, t, c, ln: (keys == t) & (ln + pos0 < c),
                          [thr, cut, lane], stage_ref)

        def cut_step(it, cut):
            cand = cut | jnp.left_shift(jnp.int32(1), n_pos_bits - 1 - it)
            return jnp.where(tied_before(cand) <= need, cand, cut)

        cut = lax.fori_loop(0, n_pos_bits, cut_step, jnp.zeros((rows, 1), I32))
        cut_b = bcast(jnp.where(tie, cut, 2 ** n_pos_bits))

        def drop(keys, pos0):
            parts = []
            for j in range(keys.shape[1] // LANES):
                blk = keys[:, j * LANES:(j + 1) * LANES]
                surplus = (blk == thr_b) & (lane + (pos0 + j * LANES) >= cut_b)
                parts.append(jnp.where(surplus, NEG_INF_KEY, blk))
            return jnp.concatenate(parts, axis=-1)

        update(drop)

    return jnp.maximum(thr, NEG_INF_KEY + 1)


def _softmax_step(s, m_ref, l_ref, acc_ref, v, c):
    m_old = m_ref[c]
    m_new = jnp.maximum(m_old, jnp.max(s, axis=-1, keepdims=True))
    alpha = jnp.exp2(m_old - m_new)
    p = jnp.exp2(s - jnp.tile(m_new, (1, s.shape[1] // LANES)))
    l_ref[c] = alpha * l_ref[c] + jnp.sum(p, axis=-1, keepdims=True)
    acc_ref[c] = alpha * acc_ref[c] + _dot(p.astype(BF16), v)
    m_ref[c] = m_new


def _attn_prompt_kernel(qi_ref, kw_ref, q_ref, ki_ref, k_ref, v_ref, o_ref,
                        keys_ref, qh_ref, wb_ref, stage_ref, m_ref, l_ref, acc_ref,
                        *, qb, sc, k_top, index_scale):
    i = pl.program_id(1)
    n_pos_bits = (keys_ref.shape[0] * sc).bit_length()
    n_chunks = ((i + 1) * qb + sc - 1) // sc

    lane = lax.broadcasted_iota(I32, (qb, LANES), 1)
    for h in range(N_IDX_HEADS):
        blk = qi_ref[:, (h // 2) * LANES:(h // 2 + 1) * LANES]
        keep = (lane < IDX_DIM) if h % 2 == 0 else (lane >= IDX_DIM)
        qh_ref[h] = jnp.where(keep, blk, jnp.zeros_like(blk))
        w_h = kw_ref[:, IDX_DIM + h:IDX_DIM + h + 1] * index_scale
        wb_ref[h] = jnp.broadcast_to(w_h, (qb, LANES))

    q_pos = i * qb + lax.broadcasted_iota(I32, (qb, sc), 0)

    def score_chunk(c, carry):
        ks = ki_ref[pl.ds(pl.multiple_of(c * sc, sc), sc), :]
        parts = [jnp.zeros((qb, LANES), F32) for _ in range(sc // LANES)]
        for h in range(N_IDX_HEADS):
            logit = _dot_nt(qh_ref[h], ks)
            w_b = wb_ref[h]
            for j in range(sc // LANES):
                parts[j] = parts[j] + jnp.maximum(logit[:, j * LANES:(j + 1) * LANES], 0.0) * w_b
        score = jnp.concatenate(parts, axis=-1)
        s_pos = c * sc + lax.broadcasted_iota(I32, (qb, sc), 1)
        keys_ref[c] = jnp.where(s_pos <= q_pos, _sort_key(score), NEG_INF_KEY)
        return carry

    lax.fori_loop(0, n_chunks, score_chunk, 0)

    def fold(f, init):
        return lax.fori_loop(0, n_chunks, lambda c, carry: f(carry, keys_ref[c], c * sc), init)

    def update(f):
        def body(c, carry):
            keys_ref[c] = f(keys_ref[c], c * sc)
            return carry
        lax.fori_loop(0, n_chunks, body, 0)

    thr = _topk_threshold(fold, update, qb, k_top, n_pos_bits, stage_ref)
    thr_b = jnp.broadcast_to(thr, (qb, LANES))

    rows = N_GROUP * qb
    m_ref[...] = jnp.full(m_ref.shape, MASK_BIAS, F32)
    l_ref[...] = jnp.zeros(l_ref.shape, F32)
    acc_ref[...] = jnp.zeros(acc_ref.shape, F32)

    def attn_chunk(c, carry):
        keys = keys_ref[c]
        bias = jnp.concatenate(
            [jnp.where(keys[:, j * LANES:(j + 1) * LANES] >= thr_b, 0.0, MASK_BIAS)
             for j in range(sc // LANES)], axis=-1)
        bias = jnp.concatenate([bias] * N_GROUP, axis=0)
        start = pl.multiple_of(c * sc, sc)
        for kv in range(N_KV_HEADS):
            qc = jnp.concatenate(
                [q_ref[:, (kv * N_GROUP + g) * HEAD_DIM:(kv * N_GROUP + g + 1) * HEAD_DIM]
                 for g in range(N_GROUP)], axis=0)
            kc = k_ref[pl.ds(start, sc), kv * HEAD_DIM:(kv + 1) * HEAD_DIM]
            vc = v_ref[pl.ds(start, sc), kv * HEAD_DIM:(kv + 1) * HEAD_DIM]
            _softmax_step(_dot_nt(qc, kc) + bias, m_ref, l_ref, acc_ref, vc, kv)
        return carry

    lax.fori_loop(0, n_chunks, attn_chunk, 0)

    for kv in range(N_KV_HEADS):
        o = acc_ref[kv] / l_ref[kv]
        for g in range(N_GROUP):
            hd = kv * N_GROUP + g
            o_ref[:, hd * HEAD_DIM:(hd + 1) * HEAD_DIM] = o[g * qb:(g + 1) * qb, :].astype(o_ref.dtype)


def _attn_prompt(qi, kw, q, ki2, kb, vb, qb, sc):
    B, S, _ = q.shape
    k_top = min(TOPK_MAX, S // 4)
    assert sc >= k_top and S % sc == 0 and S % qb == 0 and sc % qb == 0
    qblk = lambda n: pl.BlockSpec((None, qb, n), lambda b, i: (b, i, 0))
    whole = lambda n: pl.BlockSpec((None, S, n), lambda b, i: (b, 0, 0))
    rows = N_GROUP * qb
    return pl.pallas_call(
        functools.partial(_attn_prompt_kernel, qb=qb, sc=sc, k_top=k_top,
                          index_scale=(N_IDX_HEADS * IDX_DIM) ** -0.5),
        grid=(B, S // qb),
        in_specs=[qblk(qi.shape[2]), qblk(kw.shape[2]), qblk(q.shape[2]),
                  whole(ki2.shape[2]), whole(kb.shape[2]), whole(vb.shape[2])],
        out_specs=qblk(q.shape[2]),
        out_shape=jax.ShapeDtypeStruct(q.shape, BF16),
        scratch_shapes=[
            pltpu.VMEM((S // sc, qb, sc), I32),
            pltpu.VMEM((N_IDX_HEADS, qb, LANES), BF16),
            pltpu.VMEM((N_IDX_HEADS, qb, LANES), F32),
            pltpu.VMEM((3, qb, LANES), I32),
            pltpu.VMEM((N_KV_HEADS, rows, LANES), F32),
            pltpu.VMEM((N_KV_HEADS, rows, LANES), F32),
            pltpu.VMEM((N_KV_HEADS, rows, HEAD_DIM), F32),
        ],
        compiler_params=_params("parallel", "arbitrary"),
        name="attn_prompt",
    )(qi, kw, q, ki2, kb, vb)


def _attn_sample_kernel(pt_ref, qi_ref, kw_ref, q_ref, kin_ref, kn_ref, vn_ref,
                        ci_hbm, ck_hbm, cv_hbm, o_ref,
                        keys_ref, keyn_ref, ibuf, kbuf, vbuf, m_ref, l_ref, acc_ref,
                        ipg, kpg, vpg, sem_i, sem_k, sem_v,
                        *, n_pg, n_steps, t_new, page, k_top, index_scale):
    b = pl.program_id(0)
    sc = n_pg * page

    def group_copies(src, dst, sem, step, slot):
        return [pltpu.make_async_copy(src.at[pt_ref[b, step * n_pg + p]], dst.at[slot, p], sem.at[slot])
                for p in range(n_pg)]

    def start_group(src, dst, sem, step, slot):
        for cp in group_copies(src, dst, sem, step, slot):
            cp.start()

    def wait_group(src, dst, sem, step, slot):
        for cp in group_copies(src, dst, sem, step, slot):
            cp.wait()

    start_group(ci_hbm, ipg, sem_i, 0, 0)
    kv_slots = kpg.shape[0]
    for s0 in range(min(kv_slots - 1, n_steps)):
        start_group(ck_hbm, kpg, sem_k, s0, s0)
        start_group(cv_hbm, vpg, sem_v, s0, s0)

    def head_sum(x):
        return jnp.sum(x.reshape(N_IDX_HEADS, t_new, x.shape[-1]), axis=0)

    qi = qi_ref[...].astype(F32)
    lane = lax.broadcasted_iota(I32, (t_new, LANES), 1)
    pieces, wcols = [], []
    for h in range(N_IDX_HEADS):
        blk = qi[:, (h // 2) * LANES:(h // 2 + 1) * LANES]
        keep = (lane < IDX_DIM) if h % 2 == 0 else (lane >= IDX_DIM)
        pieces.append(jnp.where(keep, blk, 0.0))
        w_h = kw_ref[:, IDX_DIM + h:IDX_DIM + h + 1] * index_scale
        wcols.append(jnp.broadcast_to(w_h, (t_new, LANES)))
    lhs = jnp.concatenate(pieces, axis=0).astype(BF16)
    w_b = jnp.concatenate(wcols, axis=0)

    def scores(key_mat):
        logit = _dot(lhs, key_mat)
        parts = [head_sum(jnp.maximum(logit[:, c * LANES:(c + 1) * LANES], 0.0) * w_b)
                 for c in range(key_mat.shape[1] // LANES)]
        return jnp.concatenate(parts, axis=-1)

    def score_step(s, carry):
        slot = s % 2

        @pl.when(s + 1 < n_steps)
        def _():
            start_group(ci_hbm, ipg, sem_i, s + 1, 1 - slot)

        wait_group(ci_hbm, ipg, sem_i, s, slot)
        for p in range(n_pg):
            pg = ipg[slot, p].astype(BF16)
            ibuf[:, p * page:(p + 1) * page] = jnp.concatenate([pg, pg], axis=0)
        keys_ref[s] = _sort_key(scores(ibuf[...]))
        return carry

    lax.fori_loop(0, n_steps, score_step, 0)

    s_new = scores(kin_ref[...])
    t_pos = lax.broadcasted_iota(I32, (t_new, LANES), 0)
    s_pos = lax.broadcasted_iota(I32, (t_new, LANES), 1)
    keyn_ref[...] = jnp.where(s_pos <= t_pos, _sort_key(s_new), NEG_INF_KEY)

    past = n_steps * sc

    def fold(f, init):
        carry = init
        for c in range(n_steps):
            carry = f(carry, keys_ref[c], c * sc)
        return f(carry, keyn_ref[...], past)

    def update(f):
        for c in range(n_steps):
            keys_ref[c] = f(keys_ref[c], c * sc)
        keyn_ref[...] = f(keyn_ref[...], past)

    thr = _topk_threshold(fold, update, t_new, k_top, (past + LANES).bit_length())
    thr_b = jnp.broadcast_to(thr, (t_new, LANES))
    m_ref[...] = jnp.full(m_ref.shape, MASK_BIAS, F32)
    l_ref[...] = jnp.zeros(l_ref.shape, F32)
    acc_ref[...] = jnp.zeros(acc_ref.shape, F32)

    def attend(keys, k_of, v_of):
        bias = jnp.concatenate(
            [jnp.where(keys[:, c * LANES:(c + 1) * LANES] >= thr_b, 0.0, MASK_BIAS)
             for c in range(keys.shape[1] // LANES)], axis=-1)
        bias = jnp.concatenate([bias] * N_GROUP, axis=0)
        q = q_ref[...].astype(F32)
        for kv in range(N_KV_HEADS):
            qc = jnp.concatenate(
                [q[:, (kv * N_GROUP + g) * HEAD_DIM:(kv * N_GROUP + g + 1) * HEAD_DIM]
                 for g in range(N_GROUP)], axis=0).astype(BF16)
            _softmax_step(_dot_nt(qc, k_of(kv)) + bias, m_ref, l_ref, acc_ref, v_of(kv), kv)

    def attend_step(s, carry):
        slot = s % kv_slots
        ahead = s + kv_slots - 1

        @pl.when(ahead < n_steps)
        def _():
            start_group(ck_hbm, kpg, sem_k, ahead, ahead % kv_slots)
            start_group(cv_hbm, vpg, sem_v, ahead, ahead % kv_slots)

        wait_group(ck_hbm, kpg, sem_k, s, slot)
        wait_group(cv_hbm, vpg, sem_v, s, slot)
        for p in range(n_pg):
            for kv in range(N_KV_HEADS):
                rows_kv = pl.ds(kv, page, stride=N_KV_HEADS)
                kbuf[kv, p * page:(p + 1) * page, :] = kpg[slot, p, rows_kv, :].astype(BF16)
                vbuf[kv, p * page:(p + 1) * page, :] = vpg[slot, p, rows_kv, :].astype(BF16)
        attend(keys_ref[s], lambda kv: kbuf[kv], lambda kv: vbuf[kv])
        return carry

    lax.fori_loop(0, n_steps, attend_step, 0)

    attend(keyn_ref[...],
           lambda kv: kn_ref[:, kv * HEAD_DIM:(kv + 1) * HEAD_DIM],
           lambda kv: vn_ref[:, kv * HEAD_DIM:(kv + 1) * HEAD_DIM])
    for kv in range(N_KV_HEADS):
        o = acc_ref[kv] / l_ref[kv]
        for g in range(N_GROUP):
            hd = kv * N_GROUP + g
            o_ref[:, hd * HEAD_DIM:(hd + 1) * HEAD_DIM] = (
                o[g * t_new:(g + 1) * t_new, :].astype(o_ref.dtype))


def _attn_sample(page_table, qi, kw, q, ki2n_t, kbn, vbn, cache_ik_t, cache_k, cache_v, n_pg):
    B, T, _ = q.shape
    n_pages = page_table.shape[1]
    page = cache_ik_t.shape[2]
    assert n_pages % n_pg == 0 and page == LANES and cache_k.shape[1] == page * N_KV_HEADS
    assert cache_ik_t.shape[1] == IDX_DIM
    n_steps = n_pages // n_pg
    sc = n_pg * page
    k_top = min(TOPK_MAX, (n_pages * page + T) // 4)
    rows = N_GROUP * T

    per_b = lambda a: pl.BlockSpec((None,) + a.shape[1:], lambda b, pt: (b, 0, 0))
    hbm = pl.BlockSpec(memory_space=pl.ANY)
    grid_spec = pltpu.PrefetchScalarGridSpec(
        num_scalar_prefetch=1,
        grid=(B,),
        in_specs=[per_b(qi), per_b(kw), per_b(q), per_b(ki2n_t), per_b(kbn), per_b(vbn), hbm, hbm, hbm],
        out_specs=pl.BlockSpec((None, T, q.shape[2]), lambda b, pt: (b, 0, 0)),
        scratch_shapes=[
            pltpu.VMEM((n_steps, T, sc), I32),
            pltpu.VMEM((T, LANES), I32),
            pltpu.VMEM((2 * IDX_DIM, sc), BF16),
            pltpu.VMEM((N_KV_HEADS, sc, HEAD_DIM), BF16),
            pltpu.VMEM((N_KV_HEADS, sc, HEAD_DIM), BF16),
            pltpu.VMEM((N_KV_HEADS, rows, LANES), F32),
            pltpu.VMEM((N_KV_HEADS, rows, LANES), F32),
            pltpu.VMEM((N_KV_HEADS, rows, HEAD_DIM), F32),
            pltpu.VMEM((2, n_pg) + cache_ik_t.shape[1:], cache_ik_t.dtype),
            pltpu.VMEM((KV_SLOTS, n_pg) + cache_k.shape[1:], cache_k.dtype),
            pltpu.VMEM((KV_SLOTS, n_pg) + cache_v.shape[1:], cache_v.dtype),
            pltpu.SemaphoreType.DMA((2,)),
            pltpu.SemaphoreType.DMA((KV_SLOTS,)),
            pltpu.SemaphoreType.DMA((KV_SLOTS,)),
        ],
    )
    return pl.pallas_call(
        functools.partial(_attn_sample_kernel, n_pg=n_pg, n_steps=n_steps, t_new=T, page=page,
                          k_top=k_top, index_scale=(N_IDX_HEADS * IDX_DIM) ** -0.5),
        grid_spec=grid_spec,
        out_shape=jax.ShapeDtypeStruct(q.shape, BF16),
        compiler_params=_params("arbitrary"),
        name="attn_sample",
    )(page_table, qi, kw, q, ki2n_t, kbn, vbn, cache_ik_t, cache_k, cache_v)


def _mix_kernel(x_ref, c_ref, a_ref, w_ref, g1_ref, g2_ref, x1_ref, h_ref, *, c_conv):
    mixed = _dot(c_ref[...], w_ref[0:c_conv, :]) + _dot(a_ref[...], w_ref[c_conv:, :])
    x1 = x_ref[...] + _rms(mixed, g1_ref[...])
    x1_ref[...] = x1
    h_ref[...] = _rms(x1, g2_ref[...]).astype(BF16)


def _mix(x, conv_o, attn_o, w_out, g_post, g_pre, tm):
    T, D = x.shape
    row = lambda n: pl.BlockSpec((tm, n), lambda i: (i, 0))
    return pl.pallas_call(
        functools.partial(_mix_kernel, c_conv=conv_o.shape[1]),
        grid=(T // tm,),
        in_specs=[row(D), row(conv_o.shape[1]), row(attn_o.shape[1]),
                  _resident(w_out.shape), _resident((1, D)), _resident((1, D))],
        out_specs=(row(D), row(D)),
        out_shape=(jax.ShapeDtypeStruct((T, D), F32), jax.ShapeDtypeStruct((T, D), BF16)),
        compiler_params=_params("parallel"),
        name="mix",
    )(x, conv_o, attn_o, w_out, g_post, g_pre)


def _ffn_kernel(x_ref, h_ref, wg_ref, wu_ref, wd_ref, g_ref, y_ref, acc_ref):
    j = pl.program_id(1)

    @pl.when(j == 0)
    def _():
        acc_ref[...] = jnp.zeros(acc_ref.shape, F32)

    h = h_ref[...]
    gate = _dot(h, wg_ref[...])
    a = (gate * _sigmoid(gate) * _dot(h, wu_ref[...])).astype(BF16)
    acc_ref[...] += _dot(a, wd_ref[...])

    @pl.when(j == pl.num_programs(1) - 1)
    def _():
        y_ref[...] = x_ref[...] + _rms(acc_ref[...], g_ref[...])


def _ffn(x1, h, w_gate, w_up, w_down, g_post, tm, tf):
    T, D = x1.shape
    FF = w_gate.shape[1]
    assert FF % tf == 0
    row = lambda: pl.BlockSpec((tm, D), lambda i, j: (i, 0))
    return pl.pallas_call(
        _ffn_kernel,
        grid=(T // tm, FF // tf),
        in_specs=[row(), row(),
                  pl.BlockSpec((D, tf), lambda i, j: (0, j)),
                  pl.BlockSpec((D, tf), lambda i, j: (0, j)),
                  pl.BlockSpec((tf, D), lambda i, j: (j, 0)),
                  _resident((1, D))],
        out_specs=row(),
        out_shape=jax.ShapeDtypeStruct((T, D), F32),
        scratch_shapes=[pltpu.VMEM((tm, D), F32)],
        compiler_params=_params("parallel", "arbitrary"),
        name="ffn",
    )(x1, h, w_gate, w_up, w_down, g_post)


def _tile(n, pref):
    t = min(pref, n)
    while n % t:
        t //= 2
    return t


def _ffn_tile(ff):
    for t in (512, 256, 128):
        if ff % t == 0:
            return t
    return ff


def _dup_lanes(ki):
    return jnp.concatenate([ki, ki], axis=-1).astype(BF16)


def kernel(x_prompt, x_sample, cache_k, cache_v, cache_idx_k, state_conv, page_table,
           g_pre_mix, w_in, conv_w, conv_b, conv_ln_g, conv_ln_b, w_out,
           g_post_mix, g_pre_ffn, w_gate, w_up, w_down, g_post_ffn):
    B, S, D = x_prompt.shape
    Bd, Td, _ = x_sample.shape
    depth = w_in.shape[0]
    c_conv = conv_w.shape[2]
    c_q = N_HEADS * HEAD_DIM
    c_kv = N_KV_HEADS * HEAD_DIM
    c_qi = N_IDX_HEADS * IDX_DIM
    sizes = (c_conv, c_conv, c_q, c_kv, c_kv, c_qi, IDX_DIM, N_IDX_HEADS)
    n_in = sum(sizes)
    assert w_in.shape[2] == n_in
    n_pad = -n_in % LANES
    n_pool, page = cache_k.shape[1], cache_k.shape[2]

    xp = x_prompt.reshape(B * S, D)
    xs = x_sample.reshape(Bd * Td, D)
    outs = [[] for _ in range(8)]
    row2 = lambda a: a.reshape(1, -1)

    for l in range(depth):
        w_in_b = jnp.pad(w_in[l], ((0, 0), (0, n_pad))).astype(BF16)
        w_out_b = w_out[l].astype(BF16)
        wg_b, wu_b, wd_b = w_gate[l].astype(BF16), w_up[l].astype(BF16), w_down[l].astype(BF16)
        g_pre, g_post, g_ffn_pre, g_ffn_post = (row2(g_pre_mix[l]), row2(g_post_mix[l]),
                                                row2(g_pre_ffn[l]), row2(g_post_ffn[l]))
        cw, cb, lg, lb = conv_w[l], row2(conv_b[l]), row2(conv_ln_g[l]), row2(conv_ln_b[l])
        tf = _ffn_tile(wg_b.shape[1])

        tm = _tile(B * S, 512)
        u, q, k, v, kb, vb, qi, kw = _proj(xp, g_pre, w_in_b, sizes, tm)
        r3 = lambda a: a.reshape(B, S, a.shape[-1])
        conv_o = _conv_prompt(r3(u), cw, cb, lg, lb, _tile(S, 128))
        ki = kw[:, :IDX_DIM]
        attn_o = _attn_prompt(r3(qi), r3(kw), r3(q), r3(_dup_lanes(ki)), r3(kb), r3(vb),
                              qb=_tile(S, 256), sc=_tile(S, 512))
        x1, h2 = _mix(xp, conv_o.reshape(B * S, c_conv), attn_o.reshape(B * S, c_q),
                      w_out_b, g_post, g_ffn_pre, tm)
        xp = _ffn(x1, h2, wg_b, wu_b, wd_b, g_ffn_post, tm, tf)
        outs[0].append(k.reshape(B, S, N_KV_HEADS, HEAD_DIM))
        outs[1].append(v.reshape(B, S, N_KV_HEADS, HEAD_DIM))
        outs[2].append(ki.reshape(B, S, IDX_DIM))
        outs[3].append(r3(u)[:, S - (cw.shape[0] - 1):, :])

        tms = _tile(Bd * Td, 256)
        u, q, k, v, kb, vb, qi, kw = _proj(xs, g_pre, w_in_b, sizes, tms)
        r3 = lambda a: a.reshape(Bd, Td, a.shape[-1])
        conv_o, conv_st = _conv_sample(r3(u), state_conv[l], cw, cb, lg, lb)
        ki = kw[:, :IDX_DIM]
        pad_rows = lambda a: jnp.pad(r3(a), ((0, 0), (0, LANES - Td), (0, 0)))
        ki_t = jnp.pad(jnp.swapaxes(r3(ki), 1, 2), ((0, 0), (0, 0), (0, LANES - Td))).astype(BF16)
        attn_o = _attn_sample(
            page_table, r3(qi), r3(kw), r3(q), jnp.concatenate([ki_t, ki_t], axis=1),
            pad_rows(kb), pad_rows(vb),
            jnp.swapaxes(cache_idx_k[l], 1, 2),

            cache_k[l].reshape(n_pool, page * N_KV_HEADS, HEAD_DIM),
            cache_v[l].reshape(n_pool, page * N_KV_HEADS, HEAD_DIM),
            n_pg=_tile(page_table.shape[1], 16))
        x1, h2 = _mix(xs, conv_o.reshape(Bd * Td, c_conv), attn_o.reshape(Bd * Td, c_q),
                      w_out_b, g_post, g_ffn_pre, tms)
        xs = _ffn(x1, h2, wg_b, wu_b, wd_b, g_ffn_post, tms, tf)
        outs[4].append(k.reshape(Bd, Td, N_KV_HEADS, HEAD_DIM))
        outs[5].append(v.reshape(Bd, Td, N_KV_HEADS, HEAD_DIM))
        outs[6].append(ki.reshape(Bd, Td, IDX_DIM))
        outs[7].append(conv_st)

    return (xp.reshape(B, S, D), xs.reshape(Bd, Td, D), *[jnp.stack(o) for o in outs])
```

```python
import functools

import jax
import jax.numpy as jnp
from jax import lax
from jax.experimental import pallas as pl
from jax.experimental.pallas import tpu as pltpu

N_HEADS = 8
N_KV_HEADS = 2
N_GROUP = N_HEADS // N_KV_HEADS
HEAD_DIM = 128
N_IDX_HEADS = 16
IDX_DIM = 64
TOPK_MAX = 256
EPS = 1e-6
LANES = 128
SUBLANES = 8
VMEM_LIMIT = 56 * 1024 * 1024

INT_MIN = -2 ** 31
NEG_INF_KEY = (0xFF800000 ^ 0x7FFFFFFF) - 2 ** 32
MASK_BIAS = -1e30
COUNT_ROWS = 64
KV_SLOTS = 4
HEADS_PER_UNIT = 4
LOG2_E = 1.4426950408889634

F32 = jnp.float32
BF16 = jnp.bfloat16
I32 = jnp.int32


def _dot(a, b):
    return jnp.dot(a, b, preferred_element_type=F32)


def _dot_nt(a, b):
    return lax.dot_general(a, b, (((1,), (1,)), ((), ())), preferred_element_type=F32)


def _sigmoid(x):
    return 1.0 / (1.0 + jnp.exp(-x))


def _rms(x, g):
    return x * lax.rsqrt(jnp.mean(x * x, axis=-1, keepdims=True) + EPS) * g


def _sort_key(x):
    b = pltpu.bitcast(x, I32)
    return b ^ ((b >> 31) & 0x7FFFFFFF)


def _params(*sem):
    return pltpu.CompilerParams(dimension_semantics=sem, vmem_limit_bytes=VMEM_LIMIT)


def _resident(shape):
    nd = len(shape)
    return pl.BlockSpec(shape, lambda *_: (0,) * nd, pipeline_mode=pl.Buffered(1))


def _proj_kernel(x_ref, g_ref, w_ref, u_ref, q_ref, k_ref, v_ref, kb_ref, vb_ref,
                 qi_ref, kw_ref, *, offs, attn_scale):
    h = _rms(x_ref[...], g_ref[...]).astype(BF16)
    o_ua, o_ug, o_q, o_k, o_v, o_qi, o_kw, o_end = offs
    ua = _dot(h, w_ref[:, o_ua:o_ug])
    ug = _dot(h, w_ref[:, o_ug:o_q])
    u_ref[...] = ua * _sigmoid(ug)
    q_ref[...] = (_dot(h, w_ref[:, o_q:o_k]) * attn_scale).astype(BF16)
    k = _dot(h, w_ref[:, o_k:o_v])
    k_ref[...] = k
    kb_ref[...] = k.astype(BF16)
    v = _dot(h, w_ref[:, o_v:o_qi])
    v_ref[...] = v
    vb_ref[...] = v.astype(BF16)
    qi_ref[...] = _dot(h, w_ref[:, o_qi:o_kw]).astype(BF16)
    kw_ref[...] = _dot(h, w_ref[:, o_kw:o_end])


def _proj(x, g, w, sizes, tm):
    T, D = x.shape
    c_conv, _, c_q, c_kv, _, c_qi, c_ki, c_wi = sizes
    offs = [0]
    for s in sizes[:6]:
        offs.append(offs[-1] + s)
    offs.append(w.shape[1])
    kw_w = offs[-1] - offs[-2]
    row = lambda n: pl.BlockSpec((tm, n), lambda i: (i, 0))
    out_shape = (
        jax.ShapeDtypeStruct((T, c_conv), F32),
        jax.ShapeDtypeStruct((T, c_q), BF16),
        jax.ShapeDtypeStruct((T, c_kv), F32),
        jax.ShapeDtypeStruct((T, c_kv), F32),
        jax.ShapeDtypeStruct((T, c_kv), BF16),
        jax.ShapeDtypeStruct((T, c_kv), BF16),
        jax.ShapeDtypeStruct((T, c_qi), BF16),
        jax.ShapeDtypeStruct((T, kw_w), F32),
    )
    return pl.pallas_call(
        functools.partial(_proj_kernel, offs=tuple(offs), attn_scale=HEAD_DIM ** -0.5 * LOG2_E),
        grid=(T // tm,),
        in_specs=[row(D), _resident((1, D)), _resident(w.shape)],
        out_specs=tuple(row(s.shape[1]) for s in out_shape),
        out_shape=out_shape,
        compiler_params=_params("parallel"),
        name="proj",
    )(x, g, w)


HIST_PAD = 32


def _conv_body(full_ref, sh_ref, y_ref, w_ref, b_ref, g_ref, beta_ref, o_ref, *, width, tb, cw):
    C = full_ref.shape[1]
    base = HIST_PAD - (width - 1)
    n_b = min(SUBLANES, width)
    n_a = [(width - 1 - b) // SUBLANES + 1 for b in range(n_b)]
    for b in range(n_b):
        rows_b = tb + SUBLANES * (n_a[b] - 1)
        sh_ref[b, 0:rows_b, :] = full_ref[base + b:base + b + rows_b, :]
    for c in range(C // cw):
        cs = slice(c * cw, (c + 1) * cw)
        acc = jnp.zeros((tb, cw), F32)
        for b in range(n_b):
            for a in range(n_a[b]):
                j = b + SUBLANES * a
                acc = acc + sh_ref[b, SUBLANES * a:SUBLANES * a + tb, cs] * w_ref[j:j + 1, cs]
        y_ref[:, cs] = acc + b_ref[:, cs]
    y = y_ref[...]
    mu = jnp.mean(y, axis=-1, keepdims=True)
    yc = y - mu
    z = yc * lax.rsqrt(jnp.mean(yc * yc, axis=-1, keepdims=True) + EPS) * g_ref[...] + beta_ref[...]
    o_ref[...] = (z * _sigmoid(z)).astype(o_ref.dtype)


def _conv_scratch(tb, C, width):
    n_b = min(SUBLANES, width)
    return [pltpu.VMEM((HIST_PAD + tb, C), F32),
            pltpu.VMEM((n_b, tb + SUBLANES * ((width - 1) // SUBLANES), C), F32),
            pltpu.VMEM((tb, C), F32)]


def _conv_prompt_kernel(prev_ref, u_ref, w_ref, b_ref, g_ref, beta_ref, o_ref,
                        full_ref, sh_ref, y_ref, *, width, tb, cw):
    @pl.when(pl.program_id(1) == 0)
    def _():
        full_ref[0:HIST_PAD, :] = jnp.zeros((HIST_PAD, full_ref.shape[1]), F32)

    @pl.when(pl.program_id(1) > 0)
    def _():
        full_ref[0:HIST_PAD, :] = prev_ref[...]

    full_ref[HIST_PAD:HIST_PAD + tb, :] = u_ref[...]
    _conv_body(full_ref, sh_ref, y_ref, w_ref, b_ref, g_ref, beta_ref, o_ref, width=width, tb=tb, cw=cw)


def _conv_prompt(u, conv_w, conv_b, ln_g, ln_b, tb):
    B, S, C = u.shape
    width = conv_w.shape[0]
    r = tb // HIST_PAD
    vec = lambda: _resident((1, C))
    return pl.pallas_call(
        functools.partial(_conv_prompt_kernel, width=width, tb=tb, cw=128),
        grid=(B, S // tb),
        in_specs=[
            pl.BlockSpec((None, HIST_PAD, C), lambda b, i: (b, jnp.maximum(i * r - 1, 0), 0)),
            pl.BlockSpec((None, tb, C), lambda b, i: (b, i, 0)),
            _resident(conv_w.shape), vec(), vec(), vec(),
        ],
        out_specs=pl.BlockSpec((None, tb, C), lambda b, i: (b, i, 0)),
        out_shape=jax.ShapeDtypeStruct((B, S, C), BF16),
        scratch_shapes=_conv_scratch(tb, C, width),
        compiler_params=_params("parallel", "arbitrary"),
        name="conv_prompt",
    )(u, u, conv_w, conv_b, ln_g, ln_b)


def _conv_sample_kernel(hist_ref, u_ref, w_ref, b_ref, g_ref, beta_ref, o_ref, st_ref,
                        full_ref, sh_ref, y_ref, *, width, tb, cw):
    full_ref[0:HIST_PAD, :] = hist_ref[...]
    full_ref[HIST_PAD:HIST_PAD + tb, :] = u_ref[...]
    _conv_body(full_ref, sh_ref, y_ref, w_ref, b_ref, g_ref, beta_ref, o_ref, width=width, tb=tb, cw=cw)
    st_ref[...] = full_ref[HIST_PAD + tb - (width - 1):HIST_PAD + tb, :]


def _conv_sample(u, hist, conv_w, conv_b, ln_g, ln_b):
    B, T, C = u.shape
    width = conv_w.shape[0]
    hist_p = jnp.pad(hist, ((0, 0), (HIST_PAD - (width - 1), 0), (0, 0)))
    vec = lambda: _resident((1, C))
    return pl.pallas_call(
        functools.partial(_conv_sample_kernel, width=width, tb=T, cw=256),
        grid=(B,),
        in_specs=[
            pl.BlockSpec((None, HIST_PAD, C), lambda b: (b, 0, 0)),
            pl.BlockSpec((None, T, C), lambda b: (b, 0, 0)),
            _resident(conv_w.shape), vec(), vec(), vec(),
        ],
        out_specs=(pl.BlockSpec((None, T, C), lambda b: (b, 0, 0)),
                   pl.BlockSpec((None, width - 1, C), lambda b: (b, 0, 0))),
        out_shape=(jax.ShapeDtypeStruct((B, T, C), BF16),
                   jax.ShapeDtypeStruct((B, width - 1, C), F32)),
        scratch_shapes=_conv_scratch(T, C, width),
        compiler_params=_params("parallel"),
        name="conv_sample",
    )(hist_p, u, conv_w, conv_b, ln_g, ln_b)


def _count(fold, rows, pred, row_vals, stage_ref):
    grouped = rows > COUNT_ROWS
    if grouped:
        for n, v in enumerate(row_vals):
            stage_ref[n] = jnp.broadcast_to(v, (rows, LANES))

    def group(cnt, keys, pos0, vals):
        hits = [jnp.where(pred(keys[:, j * LANES:(j + 1) * LANES], pos0 + j * LANES, *vals), 1, 0)
                for j in range(keys.shape[1] // LANES)]
        while len(hits) > 1:
            hits = [a + b for a, b in zip(hits[::2], hits[1::2])] + hits[len(hits) & ~1:]
        return cnt + hits[0]

    def add_block(cnt, keys, pos0):
        if not grouped:
            return group(cnt, keys, pos0, [jnp.broadcast_to(v, (rows, LANES)) for v in row_vals])
        return jnp.concatenate(
            [group(cnt[r:r + COUNT_ROWS], keys[r:r + COUNT_ROWS], pos0,
                   [stage_ref[n, r:r + COUNT_ROWS, :] for n in range(len(row_vals))])
             for r in range(0, rows, COUNT_ROWS)], axis=0)

    cnt = fold(add_block, jnp.zeros((rows, LANES), I32))
    return jnp.sum(cnt.astype(F32), axis=-1, keepdims=True).astype(I32)


def _topk_threshold(fold, update, rows, k_top, n_pos_bits, stage_ref=None):
    def bcast(x):
        return jnp.broadcast_to(x, (rows, LANES))

    def count_ge(cand):
        return _count(fold, rows, lambda keys, pos0, c: keys >= c, [cand], stage_ref)

    def bit_step(it, state):
        ub, c_ub = state
        cand_u = ub | jnp.left_shift(jnp.int32(1), 31 - it)
        cnt = count_ge(cand_u ^ INT_MIN)
        keep = cnt >= k_top
        return jnp.where(keep, cand_u, ub), jnp.where(keep, cnt, c_ub)

    start = (jnp.zeros((rows, 1), I32), jnp.full((rows, 1), k_top + 1, I32))
    ub, c_thr = lax.fori_loop(0, 32, bit_step, start)
    thr = ub ^ INT_MIN
    tie = (c_thr > k_top) & (thr > NEG_INF_KEY)

    @pl.when(jnp.sum(jnp.where(tie, 1.0, 0.0)) > 0.0)
    def _():
        need = k_top - count_ge(thr + 1)
        thr_b = bcast(thr)
        lane = lax.broadcasted_iota(I32, (rows, LANES), 1)

        def tied_before(cut):
            return _count(fold, rows, lambda keys, pos0, t, c, ln: (keys == t) & (ln + pos0 < c),
                          [thr, cut, lane], stage_ref)

        def cut_step(it, cut):
            cand = cut | jnp.left_shift(jnp.int32(1), n_pos_bits - 1 - it)
            return jnp.where(tied_before(cand) <= need, cand, cut)

        cut = lax.fori_loop(0, n_pos_bits, cut_step, jnp.zeros((rows, 1), I32))
        cut_b = bcast(jnp.where(tie, cut, 2 ** n_pos_bits))

        def drop(keys, pos0):
            parts = []
            for j in range(keys.shape[1] // LANES):
                blk = keys[:, j * LANES:(j + 1) * LANES]
                surplus = (blk == thr_b) & (lane + (pos0 + j * LANES) >= cut_b)
                parts.append(jnp.where(surplus, NEG_INF_KEY, blk))
            return jnp.concatenate(parts, axis=-1)

        update(drop)

    return jnp.maximum(thr, NEG_INF_KEY + 1)


def _softmax_step(s, m_ref, l_ref, acc_ref, v, c):
    m_old = m_ref[c]
    m_new = jnp.maximum(m_old, jnp.max(s, axis=-1, keepdims=True))
    alpha = jnp.exp2(m_old - m_new)
    p = jnp.exp2(s - jnp.tile(m_new, (1, s.shape[1] // LANES)))
    l_ref[c] = alpha * l_ref[c] + jnp.sum(p, axis=-1, keepdims=True)
    acc_ref[c] = alpha * acc_ref[c] + _dot(p.astype(BF16), v)
    m_ref[c] = m_new


def _attn_prompt_kernel(qi_ref, kw_ref, q_ref, ki_ref, k_ref, v_ref, o_ref,
                        keys_ref, qh_ref, wb_ref, stage_ref, m_ref, l_ref, acc_ref,
                        *, qb, sc, k_top, index_scale):
    i = pl.program_id(1)
    n_pos_bits = (keys_ref.shape[0] * sc).bit_length()
    n_chunks = ((i + 1) * qb + sc - 1) // sc

    lane = lax.broadcasted_iota(I32, (qb, LANES), 1)
    for h in range(N_IDX_HEADS):
        blk = qi_ref[:, (h // 2) * LANES:(h // 2 + 1) * LANES]
        keep = (lane < IDX_DIM) if h % 2 == 0 else (lane >= IDX_DIM)
        qh_ref[h] = jnp.where(keep, blk, jnp.zeros_like(blk))
        w_h = kw_ref[:, IDX_DIM + h:IDX_DIM + h + 1] * index_scale
        wb_ref[h] = jnp.broadcast_to(w_h, (qb, LANES))

    q_pos = i * qb + lax.broadcasted_iota(I32, (qb, sc), 0)

    def score_chunk(c, carry):
        ks = ki_ref[pl.ds(pl.multiple_of(c * sc, sc), sc), :]
        parts = [jnp.zeros((qb, LANES), F32) for _ in range(sc // LANES)]
        for h in range(N_IDX_HEADS):
            logit = _dot_nt(qh_ref[h], ks)
            w_b = wb_ref[h]
            for j in range(sc // LANES):
                parts[j] = parts[j] + jnp.maximum(logit[:, j * LANES:(j + 1) * LANES], 0.0) * w_b
        score = jnp.concatenate(parts, axis=-1)
        s_pos = c * sc + lax.broadcasted_iota(I32, (qb, sc), 1)
        keys_ref[c] = jnp.where(s_pos <= q_pos, _sort_key(score), NEG_INF_KEY)
        return carry

    lax.fori_loop(0, n_chunks, score_chunk, 0)

    def fold(f, init):
        return lax.fori_loop(0, n_chunks, lambda c, carry: f(carry, keys_ref[c], c * sc), init)

    def update(f):
        def body(c, carry):
            keys_ref[c] = f(keys_ref[c], c * sc)
            return carry
        lax.fori_loop(0, n_chunks, body, 0)

    thr = _topk_threshold(fold, update, qb, k_top, n_pos_bits, stage_ref)
    thr_b = jnp.broadcast_to(thr, (qb, LANES))

    units = [(kv, g0) for kv in range(N_KV_HEADS) for g0 in range(0, N_GROUP, HEADS_PER_UNIT)]
    m_ref[...] = jnp.full(m_ref.shape, MASK_BIAS, F32)
    l_ref[...] = jnp.zeros(l_ref.shape, F32)
    acc_ref[...] = jnp.zeros(acc_ref.shape, F32)

    def attn_chunk(c, carry):
        keys = keys_ref[c]
        bias = jnp.concatenate(
            [jnp.where(keys[:, j * LANES:(j + 1) * LANES] >= thr_b, 0.0, MASK_BIAS)
             for j in range(sc // LANES)], axis=-1)
        bias = jnp.concatenate([bias] * HEADS_PER_UNIT, axis=0)
        start = pl.multiple_of(c * sc, sc)
        for u, (kv, g0) in enumerate(units):
            qc = jnp.concatenate(
                [q_ref[:, (kv * N_GROUP + g) * HEAD_DIM:(kv * N_GROUP + g + 1) * HEAD_DIM]
                 for g in range(g0, g0 + HEADS_PER_UNIT)], axis=0)
            kc = k_ref[pl.ds(start, sc), kv * HEAD_DIM:(kv + 1) * HEAD_DIM]
            vc = v_ref[pl.ds(start, sc), kv * HEAD_DIM:(kv + 1) * HEAD_DIM]
            _softmax_step(_dot_nt(qc, kc) + bias, m_ref, l_ref, acc_ref, vc, u)
        return carry

    lax.fori_loop(0, n_chunks, attn_chunk, 0)

    for u, (kv, g0) in enumerate(units):
        o = acc_ref[u] / l_ref[u]
        for j in range(HEADS_PER_UNIT):
            hd = kv * N_GROUP + g0 + j
            o_ref[:, hd * HEAD_DIM:(hd + 1) * HEAD_DIM] = o[j * qb:(j + 1) * qb, :].astype(o_ref.dtype)


def _attn_prompt(qi, kw, q, ki2, kb, vb, qb, sc):
    B, S, _ = q.shape
    k_top = min(TOPK_MAX, S // 4)
    assert sc >= k_top and S % sc == 0 and S % qb == 0 and sc % qb == 0
    qblk = lambda n: pl.BlockSpec((None, qb, n), lambda b, i: (b, i, 0))
    whole = lambda n: pl.BlockSpec((None, S, n), lambda b, i: (b, 0, 0), pipeline_mode=pl.Buffered(1))
    rows = HEADS_PER_UNIT * qb
    n_units = N_HEADS // HEADS_PER_UNIT
    return pl.pallas_call(
        functools.partial(_attn_prompt_kernel, qb=qb, sc=sc, k_top=k_top,
                          index_scale=(N_IDX_HEADS * IDX_DIM) ** -0.5),
        grid=(B, S // qb),
        in_specs=[qblk(qi.shape[2]), qblk(kw.shape[2]), qblk(q.shape[2]),
                  whole(ki2.shape[2]), whole(kb.shape[2]), whole(vb.shape[2])],
        out_specs=qblk(q.shape[2]),
        out_shape=jax.ShapeDtypeStruct(q.shape, BF16),
        scratch_shapes=[
            pltpu.VMEM((S // sc, qb, sc), I32),
            pltpu.VMEM((N_IDX_HEADS, qb, LANES), BF16),
            pltpu.VMEM((N_IDX_HEADS, qb, LANES), F32),
            pltpu.VMEM((3, qb, LANES), I32),
            pltpu.VMEM((n_units, rows, LANES), F32),
            pltpu.VMEM((n_units, rows, LANES), F32),
            pltpu.VMEM((n_units, rows, HEAD_DIM), F32),
        ],
        compiler_params=_params("parallel", "arbitrary"),
        name="attn_prompt",
    )(qi, kw, q, ki2, kb, vb)


def _attn_sample_kernel(pt_ref, qi_ref, kw_ref, q_ref, kin_ref, kn_ref, vn_ref,
                        ci_hbm, ck_hbm, cv_hbm, o_ref,
                        keys_ref, keyn_ref, ibuf, kbuf, vbuf, m_ref, l_ref, acc_ref,
                        ipg, kpg, vpg, sem_i, sem_k, sem_v,
                        *, n_pg, n_steps, t_new, page, k_top, index_scale):
    b = pl.program_id(0)
    sc = n_pg * page

    def group_copies(src, dst, sem, step, slot):
        return [pltpu.make_async_copy(src.at[pt_ref[b, step * n_pg + p]], dst.at[slot, p], sem.at[slot])
                for p in range(n_pg)]

    def start_group(src, dst, sem, step, slot):
        for cp in group_copies(src, dst, sem, step, slot):
            cp.start()

    def wait_group(src, dst, sem, step, slot):
        for cp in group_copies(src, dst, sem, step, slot):
            cp.wait()

    start_group(ci_hbm, ipg, sem_i, 0, 0)
    kv_slots = kpg.shape[0]
    for s0 in range(min(kv_slots - 1, n_steps)):
        start_group(ck_hbm, kpg, sem_k, s0, s0)
        start_group(cv_hbm, vpg, sem_v, s0, s0)

    def head_sum(x):
        return jnp.sum(x.reshape(N_IDX_HEADS, t_new, x.shape[-1]), axis=0)

    qi = qi_ref[...].astype(F32)
    lane = lax.broadcasted_iota(I32, (t_new, LANES), 1)
    pieces, wcols = [], []
    for h in range(N_IDX_HEADS):
        blk = qi[:, (h // 2) * LANES:(h // 2 + 1) * LANES]
        keep = (lane < IDX_DIM) if h % 2 == 0 else (lane >= IDX_DIM)
        pieces.append(jnp.where(keep, blk, 0.0))
        w_h = kw_ref[:, IDX_DIM + h:IDX_DIM + h + 1] * index_scale
        wcols.append(jnp.broadcast_to(w_h, (t_new, LANES)))
    lhs = jnp.concatenate(pieces, axis=0).astype(BF16)
    w_b = jnp.concatenate(wcols, axis=0)

    def scores(key_mat):
        logit = _dot(lhs, key_mat)
        parts = [head_sum(jnp.maximum(logit[:, c * LANES:(c + 1) * LANES], 0.0) * w_b)
                 for c in range(key_mat.shape[1] // LANES)]
        return jnp.concatenate(parts, axis=-1)

    def score_step(s, carry):
        slot = s % 2

        @pl.when(s + 1 < n_steps)
        def _():
            start_group(ci_hbm, ipg, sem_i, s + 1, 1 - slot)

        wait_group(ci_hbm, ipg, sem_i, s, slot)
        for p in range(n_pg):
            pg = ipg[slot, p].astype(BF16)
            ibuf[:, p * page:(p + 1) * page] = jnp.concatenate([pg, pg], axis=0)
        keys_ref[s] = _sort_key(scores(ibuf[...]))
        return carry

    lax.fori_loop(0, n_steps, score_step, 0)

    s_new = scores(kin_ref[...])
    t_pos = lax.broadcasted_iota(I32, (t_new, LANES), 0)
    s_pos = lax.broadcasted_iota(I32, (t_new, LANES), 1)
    keyn_ref[...] = jnp.where(s_pos <= t_pos, _sort_key(s_new), NEG_INF_KEY)

    past = n_steps * sc

    def fold(f, init):
        carry = init
        for c in range(n_steps):
            carry = f(carry, keys_ref[c], c * sc)
        return f(carry, keyn_ref[...], past)

    def update(f):
        for c in range(n_steps):
            keys_ref[c] = f(keys_ref[c], c * sc)
        keyn_ref[...] = f(keyn_ref[...], past)

    thr = _topk_threshold(fold, update, t_new, k_top, (past + LANES).bit_length())
    thr_b = jnp.broadcast_to(thr, (t_new, LANES))
    m_ref[...] = jnp.full(m_ref.shape, MASK_BIAS, F32)
    l_ref[...] = jnp.zeros(l_ref.shape, F32)
    acc_ref[...] = jnp.zeros(acc_ref.shape, F32)

    def attend(keys, k_of, v_of):
        bias = jnp.concatenate(
            [jnp.where(keys[:, c * LANES:(c + 1) * LANES] >= thr_b, 0.0, MASK_BIAS)
             for c in range(keys.shape[1] // LANES)], axis=-1)
        bias = jnp.concatenate([bias] * N_GROUP, axis=0)
        q = q_ref[...].astype(F32)
        for kv in range(N_KV_HEADS):
            qc = jnp.concatenate(
                [q[:, (kv * N_GROUP + g) * HEAD_DIM:(kv * N_GROUP + g + 1) * HEAD_DIM]
                 for g in range(N_GROUP)], axis=0).astype(BF16)
            _softmax_step(_dot_nt(qc, k_of(kv)) + bias, m_ref, l_ref, acc_ref, v_of(kv), kv)

    def attend_step(s, carry):
        slot = s % kv_slots
        ahead = s + kv_slots - 1

        @pl.when(ahead < n_steps)
        def _():
            start_group(ck_hbm, kpg, sem_k, ahead, ahead % kv_slots)
            start_group(cv_hbm, vpg, sem_v, ahead, ahead % kv_slots)

        wait_group(ck_hbm, kpg, sem_k, s, slot)
        wait_group(cv_hbm, vpg, sem_v, s, slot)
        for p in range(n_pg):
            for kv in range(N_KV_HEADS):
                rows_kv = pl.ds(kv, page, stride=N_KV_HEADS)
                kbuf[kv, p * page:(p + 1) * page, :] = kpg[slot, p, rows_kv, :].astype(BF16)
                vbuf[kv, p * page:(p + 1) * page, :] = vpg[slot, p, rows_kv, :].astype(BF16)
        attend(keys_ref[s], lambda kv: kbuf[kv], lambda kv: vbuf[kv])
        return carry

    lax.fori_loop(0, n_steps, attend_step, 0)

    attend(keyn_ref[...],
           lambda kv: kn_ref[:, kv * HEAD_DIM:(kv + 1) * HEAD_DIM],
           lambda kv: vn_ref[:, kv * HEAD_DIM:(kv + 1) * HEAD_DIM])
    for kv in range(N_KV_HEADS):
        o = acc_ref[kv] / l_ref[kv]
        for g in range(N_GROUP):
            hd = kv * N_GROUP + g
            o_ref[:, hd * HEAD_DIM:(hd + 1) * HEAD_DIM] = (
                o[g * t_new:(g + 1) * t_new, :].astype(o_ref.dtype))


def _attn_sample(page_table, qi, kw, q, ki2n_t, kbn, vbn, cache_ik_t, cache_k, cache_v, n_pg):
    B, T, _ = q.shape
    n_pages = page_table.shape[1]
    page = cache_ik_t.shape[2]
    assert n_pages % n_pg == 0 and page == LANES and cache_k.shape[1] == page * N_KV_HEADS
    assert cache_ik_t.shape[1] == IDX_DIM
    n_steps = n_pages // n_pg
    sc = n_pg * page
    k_top = min(TOPK_MAX, (n_pages * page + T) // 4)
    rows = N_GROUP * T

    per_b = lambda a: pl.BlockSpec((None,) + a.shape[1:], lambda b, pt: (b, 0, 0))
    hbm = pl.BlockSpec(memory_space=pl.ANY)
    grid_spec = pltpu.PrefetchScalarGridSpec(
        num_scalar_prefetch=1,
        grid=(B,),
        in_specs=[per_b(qi), per_b(kw), per_b(q), per_b(ki2n_t), per_b(kbn), per_b(vbn), hbm, hbm, hbm],
        out_specs=pl.BlockSpec((None, T, q.shape[2]), lambda b, pt: (b, 0, 0)),
        scratch_shapes=[
            pltpu.VMEM((n_steps, T, sc), I32),
            pltpu.VMEM((T, LANES), I32),
            pltpu.VMEM((2 * IDX_DIM, sc), BF16),
            pltpu.VMEM((N_KV_HEADS, sc, HEAD_DIM), BF16),
            pltpu.VMEM((N_KV_HEADS, sc, HEAD_DIM), BF16),
            pltpu.VMEM((N_KV_HEADS, rows, LANES), F32),
            pltpu.VMEM((N_KV_HEADS, rows, LANES), F32),
            pltpu.VMEM((N_KV_HEADS, rows, HEAD_DIM), F32),
            pltpu.VMEM((2, n_pg) + cache_ik_t.shape[1:], cache_ik_t.dtype),
            pltpu.VMEM((KV_SLOTS, n_pg) + cache_k.shape[1:], cache_k.dtype),
            pltpu.VMEM((KV_SLOTS, n_pg) + cache_v.shape[1:], cache_v.dtype),
            pltpu.SemaphoreType.DMA((2,)),
            pltpu.SemaphoreType.DMA((KV_SLOTS,)),
            pltpu.SemaphoreType.DMA((KV_SLOTS,)),
        ],
    )
    return pl.pallas_call(
        functools.partial(_attn_sample_kernel, n_pg=n_pg, n_steps=n_steps, t_new=T, page=page,
                          k_top=k_top, index_scale=(N_IDX_HEADS * IDX_DIM) ** -0.5),
        grid_spec=grid_spec,
        out_shape=jax.ShapeDtypeStruct(q.shape, BF16),
        compiler_params=_params("arbitrary"),
        name="attn_sample",
    )(page_table, qi, kw, q, ki2n_t, kbn, vbn, cache_ik_t, cache_k, cache_v)


def _mix_kernel(x_ref, c_ref, a_ref, w_ref, g1_ref, g2_ref, x1_ref, h_ref, *, c_conv):
    tm = x_ref.shape[0]
    for r0 in range(0, tm, tm // 2):
        rs = slice(r0, r0 + tm // 2)
        mixed = _dot(c_ref[rs, :], w_ref[0:c_conv, :]) + _dot(a_ref[rs, :], w_ref[c_conv:, :])
        x1 = x_ref[rs, :] + _rms(mixed, g1_ref[...])
        x1_ref[rs, :] = x1
        h_ref[rs, :] = _rms(x1, g2_ref[...]).astype(BF16)


def _mix(x, conv_o, attn_o, w_out, g_post, g_pre, tm):
    T, D = x.shape
    row = lambda n: pl.BlockSpec((tm, n), lambda i: (i, 0))
    return pl.pallas_call(
        functools.partial(_mix_kernel, c_conv=conv_o.shape[1]),
        grid=(T // tm,),
        in_specs=[row(D), row(conv_o.shape[1]), row(attn_o.shape[1]),
                  _resident(w_out.shape), _resident((1, D)), _resident((1, D))],
        out_specs=(row(D), row(D)),
        out_shape=(jax.ShapeDtypeStruct((T, D), F32), jax.ShapeDtypeStruct((T, D), BF16)),
        compiler_params=_params("parallel"),
        name="mix",
    )(x, conv_o, attn_o, w_out, g_post, g_pre)


def _ffn_kernel(x_ref, h_ref, wg_ref, wu_ref, wd_ref, g_ref, y_ref, acc_ref):
    j = pl.program_id(1)

    @pl.when(j == 0)
    def _():
        acc_ref[...] = jnp.zeros(acc_ref.shape, F32)

    h = h_ref[...]
    gate = _dot(h, wg_ref[...])
    a = (gate * _sigmoid(gate) * _dot(h, wu_ref[...])).astype(BF16)
    acc_ref[...] += _dot(a, wd_ref[...])

    @pl.when(j == pl.num_programs(1) - 1)
    def _():
        y_ref[...] = x_ref[...] + _rms(acc_ref[...], g_ref[...])


def _ffn(x1, h, w_gate, w_up, w_down, g_post, tm, tf):
    T, D = x1.shape
    FF = w_gate.shape[1]
    assert FF % tf == 0
    row = lambda: pl.BlockSpec((tm, D), lambda i, j: (i, 0))
    return pl.pallas_call(
        _ffn_kernel,
        grid=(T // tm, FF // tf),
        in_specs=[row(), row(),
                  pl.BlockSpec((D, tf), lambda i, j: (0, j)),
                  pl.BlockSpec((D, tf), lambda i, j: (0, j)),
                  pl.BlockSpec((tf, D), lambda i, j: (j, 0)),
                  _resident((1, D))],
        out_specs=row(),
        out_shape=jax.ShapeDtypeStruct((T, D), F32),
        scratch_shapes=[pltpu.VMEM((tm, D), F32)],
        compiler_params=_params("parallel", "arbitrary"),
        name="ffn",
    )(x1, h, w_gate, w_up, w_down, g_post)


def _tile(n, pref):
    t = min(pref, n)
    while n % t:
        t //= 2
    return t


def _ffn_tile(ff):
    for t in (512, 256, 128):
        if ff % t == 0:
            return t
    return ff


def _dup_lanes(ki):
    return jnp.concatenate([ki, ki], axis=-1).astype(BF16)


def kernel(x_prompt, x_sample, cache_k, cache_v, cache_idx_k, state_conv, page_table,
           g_pre_mix, w_in, conv_w, conv_b, conv_ln_g, conv_ln_b, w_out,
           g_post_mix, g_pre_ffn, w_gate, w_up, w_down, g_post_ffn):
    B, S, D = x_prompt.shape
    Bd, Td, _ = x_sample.shape
    depth = w_in.shape[0]
    c_conv = conv_w.shape[2]
    c_q = N_HEADS * HEAD_DIM
    c_kv = N_KV_HEADS * HEAD_DIM
    c_qi = N_IDX_HEADS * IDX_DIM
    sizes = (c_conv, c_conv, c_q, c_kv, c_kv, c_qi, IDX_DIM, N_IDX_HEADS)
    n_in = sum(sizes)
    assert w_in.shape[2] == n_in
    n_pad = -n_in % LANES
    n_pool, page = cache_k.shape[1], cache_k.shape[2]

    xp = x_prompt.reshape(B * S, D)
    xs = x_sample.reshape(Bd * Td, D)
    outs = [[] for _ in range(8)]
    row2 = lambda a: a.reshape(1, -1)

    for l in range(depth):
        w_in_b = jnp.pad(w_in[l], ((0, 0), (0, n_pad))).astype(BF16)
        w_out_b = w_out[l].astype(BF16)
        wg_b, wu_b, wd_b = w_gate[l].astype(BF16), w_up[l].astype(BF16), w_down[l].astype(BF16)
        g_pre, g_post, g_ffn_pre, g_ffn_post = (row2(g_pre_mix[l]), row2(g_post_mix[l]),
                                                row2(g_pre_ffn[l]), row2(g_post_ffn[l]))
        cw, cb, lg, lb = conv_w[l], row2(conv_b[l]), row2(conv_ln_g[l]), row2(conv_ln_b[l])
        tf = _ffn_tile(wg_b.shape[1])

        tm = _tile(B * S, 512)
        u, q, k, v, kb, vb, qi, kw = _proj(xp, g_pre, w_in_b, sizes, tm)
        r3 = lambda a: a.reshape(B, S, a.shape[-1])
        conv_o = _conv_prompt(r3(u), cw, cb, lg, lb, _tile(S, 256))
        ki = kw[:, :IDX_DIM]
        attn_o = _attn_prompt(r3(qi), r3(kw), r3(q), r3(_dup_lanes(ki)), r3(kb), r3(vb),
                              qb=_tile(S, 512), sc=_tile(S, 512))
        x1, h2 = _mix(xp, conv_o.reshape(B * S, c_conv), attn_o.reshape(B * S, c_q),
                      w_out_b, g_post, g_ffn_pre, tm)
        xp = _ffn(x1, h2, wg_b, wu_b, wd_b, g_ffn_post, tm, tf)
        outs[0].append(k.reshape(B, S, N_KV_HEADS, HEAD_DIM))
        outs[1].append(v.reshape(B, S, N_KV_HEADS, HEAD_DIM))
        outs[2].append(ki.reshape(B, S, IDX_DIM))
        outs[3].append(r3(u)[:, S - (cw.shape[0] - 1):, :])

        tms = _tile(Bd * Td, 256)
        u, q, k, v, kb, vb, qi, kw = _proj(xs, g_pre, w_in_b, sizes, tms)
        r3 = lambda a: a.reshape(Bd, Td, a.shape[-1])
        conv_o, conv_st = _conv_sample(r3(u), state_conv[l], cw, cb, lg, lb)
        ki = kw[:, :IDX_DIM]
        pad_rows = lambda a: jnp.pad(r3(a), ((0, 0), (0, LANES - Td), (0, 0)))
        ki_t = jnp.pad(jnp.swapaxes(r3(ki), 1, 2), ((0, 0), (0, 0), (0, LANES - Td))).astype(BF16)
        attn_o = _attn_sample(
            page_table, r3(qi), r3(kw), r3(q), jnp.concatenate([ki_t, ki_t], axis=1),
            pad_rows(kb), pad_rows(vb),
            jnp.swapaxes(cache_idx_k[l], 1, 2),

            cache_k[l].reshape(n_pool, page * N_KV_HEADS, HEAD_DIM),
            cache_v[l].reshape(n_pool, page * N_KV_HEADS, HEAD_DIM),
            n_pg=_tile(page_table.shape[1], 16))
        x1, h2 = _mix(xs, conv_o.reshape(Bd * Td, c_conv), attn_o.reshape(Bd * Td, c_q),
                      w_out_b, g_post, g_ffn_pre, tms)
        xs = _ffn(x1, h2, wg_b, wu_b, wd_b, g_ffn_post, tms, tf)
        outs[4].append(k.reshape(Bd, Td, N_KV_HEADS, HEAD_DIM))
        outs[5].append(v.reshape(Bd, Td, N_KV_HEADS, HEAD_DIM))
        outs[6].append(ki.reshape(Bd, Td, IDX_DIM))
        outs[7].append(conv_st)

    return (xp.reshape(B, S, D), xs.reshape(Bd, Td, D), *[jnp.stack(o) for o in outs])
```

```python
import functools

import jax
import jax.numpy as jnp
from jax import lax
from jax.experimental import pallas as pl
from jax.experimental.pallas import tpu as pltpu

N_HEADS = 8
N_KV_HEADS = 2
N_GROUP = N_HEADS // N_KV_HEADS
HEAD_DIM = 128
N_IDX_HEADS = 16
IDX_DIM = 64
TOPK_MAX = 256
EPS = 1e-6
LANES = 128
SUBLANES = 8
VMEM_LIMIT = 56 * 1024 * 1024

INT_MIN = -2 ** 31
NEG_INF_KEY = (0xFF800000 ^ 0x7FFFFFFF) - 2 ** 32
MASK_BIAS = -1e30
COUNT_ROWS = 64
KV_SLOTS = 8
HEADS_PER_UNIT = 4
LOG2_E = 1.4426950408889634

F32 = jnp.float32
BF16 = jnp.bfloat16
I32 = jnp.int32


def _dot(a, b):
    return jnp.dot(a, b, preferred_element_type=F32)


def _dot_nt(a, b):
    return lax.dot_general(a, b, (((1,), (1,)), ((), ())), preferred_element_type=F32)


def _sigmoid(x):
    return 1.0 / (1.0 + jnp.exp(-x))


def _rms(x, g):
    return x * lax.rsqrt(jnp.mean(x * x, axis=-1, keepdims=True) + EPS) * g


def _sort_key(x):
    b = pltpu.bitcast(x, I32)
    return b ^ ((b >> 31) & 0x7FFFFFFF)


def _params(*sem):
    return pltpu.CompilerParams(dimension_semantics=sem, vmem_limit_bytes=VMEM_LIMIT)


def _resident(shape):
    nd = len(shape)
    return pl.BlockSpec(shape, lambda *_: (0,) * nd, pipeline_mode=pl.Buffered(1))


def _proj_kernel(x_ref, g_ref, w_ref, u_ref, q_ref, k_ref, v_ref, kb_ref, vb_ref,
                 qi_ref, kw_ref, *, offs, attn_scale):
    h = _rms(x_ref[...], g_ref[...]).astype(BF16)
    o_ua, o_ug, o_q, o_k, o_v, o_qi, o_kw, o_end = offs
    ua = _dot(h, w_ref[:, o_ua:o_ug])
    ug = _dot(h, w_ref[:, o_ug:o_q])
    u_ref[...] = ua * _sigmoid(ug)
    q_ref[...] = (_dot(h, w_ref[:, o_q:o_k]) * attn_scale).astype(BF16)
    k = _dot(h, w_ref[:, o_k:o_v])
    k_ref[...] = k
    kb_ref[...] = k.astype(BF16)
    v = _dot(h, w_ref[:, o_v:o_qi])
    v_ref[...] = v
    vb_ref[...] = v.astype(BF16)
    qi_ref[...] = _dot(h, w_ref[:, o_qi:o_kw]).astype(BF16)
    kw_ref[...] = _dot(h, w_ref[:, o_kw:o_end])


def _proj(x, g, w, sizes, tm):
    T, D = x.shape
    c_conv, _, c_q, c_kv, _, c_qi, c_ki, c_wi = sizes
    offs = [0]
    for s in sizes[:6]:
        offs.append(offs[-1] + s)
    offs.append(w.shape[1])
    kw_w = offs[-1] - offs[-2]
    row = lambda n: pl.BlockSpec((tm, n), lambda i: (i, 0))
    out_shape = (
        jax.ShapeDtypeStruct((T, c_conv), F32),
        jax.ShapeDtypeStruct((T, c_q), BF16),
        jax.ShapeDtypeStruct((T, c_kv), F32),
        jax.ShapeDtypeStruct((T, c_kv), F32),
        jax.ShapeDtypeStruct((T, c_kv), BF16),
        jax.ShapeDtypeStruct((T, c_kv), BF16),
        jax.ShapeDtypeStruct((T, c_qi), BF16),
        jax.ShapeDtypeStruct((T, kw_w), F32),
    )
    return pl.pallas_call(
        functools.partial(_proj_kernel, offs=tuple(offs), attn_scale=HEAD_DIM ** -0.5 * LOG2_E),
        grid=(T // tm,),
        in_specs=[row(D), _resident((1, D)), _resident(w.shape)],
        out_specs=tuple(row(s.shape[1]) for s in out_shape),
        out_shape=out_shape,
        compiler_params=_params("parallel"),
        name="proj",
    )(x, g, w)


HIST_PAD = 32


def _conv_body(full_ref, sh_ref, y_ref, w_ref, b_ref, g_ref, beta_ref, o_ref, *, width, tb, cw):
    C = full_ref.shape[1]
    base = HIST_PAD - (width - 1)
    n_b = min(SUBLANES, width)
    n_a = [(width - 1 - b) // SUBLANES + 1 for b in range(n_b)]
    for b in range(n_b):
        rows_b = tb + SUBLANES * (n_a[b] - 1)
        sh_ref[b, 0:rows_b, :] = full_ref[base + b:base + b + rows_b, :]
    for c in range(C // cw):
        cs = slice(c * cw, (c + 1) * cw)
        acc = jnp.zeros((tb, cw), F32)
        for b in range(n_b):
            for a in range(n_a[b]):
                j = b + SUBLANES * a
                acc = acc + sh_ref[b, SUBLANES * a:SUBLANES * a + tb, cs] * w_ref[j:j + 1, cs]
        y_ref[:, cs] = acc + b_ref[:, cs]
    y = y_ref[...]
    mu = jnp.mean(y, axis=-1, keepdims=True)
    yc = y - mu
    z = yc * lax.rsqrt(jnp.mean(yc * yc, axis=-1, keepdims=True) + EPS) * g_ref[...] + beta_ref[...]
    o_ref[...] = (z * _sigmoid(z)).astype(o_ref.dtype)


def _conv_scratch(tb, C, width):
    n_b = min(SUBLANES, width)
    return [pltpu.VMEM((HIST_PAD + tb, C), F32),
            pltpu.VMEM((n_b, tb + SUBLANES * ((width - 1) // SUBLANES), C), F32),
            pltpu.VMEM((tb, C), F32)]


def _conv_prompt_kernel(prev_ref, u_ref, w_ref, b_ref, g_ref, beta_ref, o_ref,
                        full_ref, sh_ref, y_ref, *, width, tb, cw):
    @pl.when(pl.program_id(1) == 0)
    def _():
        full_ref[0:HIST_PAD, :] = jnp.zeros((HIST_PAD, full_ref.shape[1]), F32)

    @pl.when(pl.program_id(1) > 0)
    def _():
        full_ref[0:HIST_PAD, :] = prev_ref[...]

    full_ref[HIST_PAD:HIST_PAD + tb, :] = u_ref[...]
    _conv_body(full_ref, sh_ref, y_ref, w_ref, b_ref, g_ref, beta_ref, o_ref, width=width, tb=tb, cw=cw)


def _conv_prompt(u, conv_w, conv_b, ln_g, ln_b, tb):
    B, S, C = u.shape
    width = conv_w.shape[0]
    r = tb // HIST_PAD
    vec = lambda: _resident((1, C))
    return pl.pallas_call(
        functools.partial(_conv_prompt_kernel, width=width, tb=tb, cw=128),
        grid=(B, S // tb),
        in_specs=[
            pl.BlockSpec((None, HIST_PAD, C), lambda b, i: (b, jnp.maximum(i * r - 1, 0), 0)),
            pl.BlockSpec((None, tb, C), lambda b, i: (b, i, 0)),
            _resident(conv_w.shape), vec(), vec(), vec(),
        ],
        out_specs=pl.BlockSpec((None, tb, C), lambda b, i: (b, i, 0)),
        out_shape=jax.ShapeDtypeStruct((B, S, C), BF16),
        scratch_shapes=_conv_scratch(tb, C, width),
        compiler_params=_params("parallel", "arbitrary"),
        name="conv_prompt",
    )(u, u, conv_w, conv_b, ln_g, ln_b)


def _conv_sample_kernel(hist_ref, u_ref, w_ref, b_ref, g_ref, beta_ref, o_ref, st_ref,
                        full_ref, sh_ref, y_ref, *, width, tb, cw):
    full_ref[0:HIST_PAD, :] = hist_ref[...]
    full_ref[HIST_PAD:HIST_PAD + tb, :] = u_ref[...]
    _conv_body(full_ref, sh_ref, y_ref, w_ref, b_ref, g_ref, beta_ref, o_ref, width=width, tb=tb, cw=cw)
    st_ref[...] = full_ref[HIST_PAD + tb - (width - 1):HIST_PAD + tb, :]


def _conv_sample(u, hist, conv_w, conv_b, ln_g, ln_b):
    B, T, C = u.shape
    width = conv_w.shape[0]
    hist_p = jnp.pad(hist, ((0, 0), (HIST_PAD - (width - 1), 0), (0, 0)))
    vec = lambda: _resident((1, C))
    return pl.pallas_call(
        functools.partial(_conv_sample_kernel, width=width, tb=T, cw=256),
        grid=(B,),
        in_specs=[
            pl.BlockSpec((None, HIST_PAD, C), lambda b: (b, 0, 0)),
            pl.BlockSpec((None, T, C), lambda b: (b, 0, 0)),
            _resident(conv_w.shape), vec(), vec(), vec(),
        ],
        out_specs=(pl.BlockSpec((None, T, C), lambda b: (b, 0, 0)),
                   pl.BlockSpec((None, width - 1, C), lambda b: (b, 0, 0))),
        out_shape=(jax.ShapeDtypeStruct((B, T, C), BF16),
                   jax.ShapeDtypeStruct((B, width - 1, C), F32)),
        scratch_shapes=_conv_scratch(T, C, width),
        compiler_params=_params("parallel"),
        name="conv_sample",
    )(hist_p, u, conv_w, conv_b, ln_g, ln_b)


def _count(fold, rows, pred, row_vals, stage_ref):
    grouped = rows > COUNT_ROWS
    if grouped:
        for n, v in enumerate(row_vals):
            stage_ref[n] = jnp.broadcast_to(v, (rows, LANES))

    def group(cnt, keys, pos0, vals):
        hits = [jnp.where(pred(keys[:, j * LANES:(j + 1) * LANES], pos0 + j * LANES, *vals), 1, 0)
                for j in range(keys.shape[1] // LANES)]
        while len(hits) > 1:
            hits = [a + b for a, b in zip(hits[::2], hits[1::2])] + hits[len(hits) & ~1:]
        return cnt + hits[0]

    def add_block(cnt, keys, pos0):
        if not grouped:
            return group(cnt, keys, pos0, [jnp.broadcast_to(v, (rows, LANES)) for v in row_vals])
        return jnp.concatenate(
            [group(cnt[r:r + COUNT_ROWS], keys[r:r + COUNT_ROWS], pos0,
                   [stage_ref[n, r:r + COUNT_ROWS, :] for n in range(len(row_vals))])
             for r in range(0, rows, COUNT_ROWS)], axis=0)

    cnt = fold(add_block, jnp.zeros((rows, LANES), I32))
    return jnp.sum(cnt.astype(F32), axis=-1, keepdims=True).astype(I32)


def _topk_threshold(fold, update, rows, k_top, n_pos_bits, stage_ref=None):
    def bcast(x):
        return jnp.broadcast_to(x, (rows, LANES))

    def count_ge(cand):
        return _count(fold, rows, lambda keys, pos0, c: keys >= c, [cand], stage_ref)

    def bit_step(it, state):
        ub, c_ub = state
        cand_u = ub | jnp.left_shift(jnp.int32(1), 31 - it)
        cnt = count_ge(cand_u ^ INT_MIN)
        keep = cnt >= k_top
        return jnp.where(keep, cand_u, ub), jnp.where(keep, cnt, c_ub)

    start = (jnp.zeros((rows, 1), I32), jnp.full((rows, 1), k_top + 1, I32))
    ub, c_thr = lax.fori_loop(0, 32, bit_step, start)
    thr = ub ^ INT_MIN
    tie = (c_thr > k_top) & (thr > NEG_INF_KEY)

    @pl.when(jnp.sum(jnp.where(tie, 1.0, 0.0)) > 0.0)
    def _():
        need = k_top - count_ge(thr + 1)
        thr_b = bcast(thr)
        lane = lax.broadcasted_iota(I32, (rows, LANES), 1)

        def tied_before(cut):
            return _count(fold, rows, lambda keys, pos0, t, c, ln: (keys == t) & (ln + pos0 < c),
                          [thr, cut, lane], stage_ref)

        def cut_step(it, cut):
            cand = cut | jnp.left_shift(jnp.int32(1), n_pos_bits - 1 - it)
            return jnp.where(tied_before(cand) <= need, cand, cut)

        cut = lax.fori_loop(0, n_pos_bits, cut_step, jnp.zeros((rows, 1), I32))
        cut_b = bcast(jnp.where(tie, cut, 2 ** n_pos_bits))

        def drop(keys, pos0):
            parts = []
            for j in range(keys.shape[1] // LANES):
                blk = keys[:, j * LANES:(j + 1) * LANES]
                surplus = (blk == thr_b) & (lane + (pos0 + j * LANES) >= cut_b)
                parts.append(jnp.where(surplus, NEG_INF_KEY, blk))
            return jnp.concatenate(parts, axis=-1)

        update(drop)

    return jnp.maximum(thr, NEG_INF_KEY + 1)


def _softmax_step(s, m_ref, l_ref, acc_ref, v, c):
    m_old = m_ref[c]
    m_new = jnp.maximum(m_old, jnp.max(s, axis=-1, keepdims=True))
    alpha = jnp.exp2(m_old - m_new)
    p = jnp.exp2(s - jnp.tile(m_new, (1, s.shape[1] // LANES)))
    l_ref[c] = alpha * l_ref[c] + jnp.sum(p, axis=-1, keepdims=True)
    acc_ref[c] = alpha * acc_ref[c] + _dot(p.astype(BF16), v)
    m_ref[c] = m_new


def _attn_prompt_kernel(qi_ref, kw_ref, q_ref, ki_ref, k_ref, v_ref, o_ref,
                        keys_ref, qh_ref, wb_ref, stage_ref, m_ref, l_ref, acc_ref,
                        *, qb, sc, k_top, index_scale):
    i = pl.program_id(1)
    n_pos_bits = (keys_ref.shape[0] * sc).bit_length()
    n_chunks = ((i + 1) * qb + sc - 1) // sc

    lane = lax.broadcasted_iota(I32, (qb, LANES), 1)
    for h in range(N_IDX_HEADS):
        blk = qi_ref[:, (h // 2) * LANES:(h // 2 + 1) * LANES]
        keep = (lane < IDX_DIM) if h % 2 == 0 else (lane >= IDX_DIM)
        qh_ref[h] = jnp.where(keep, blk, jnp.zeros_like(blk))
        w_h = kw_ref[:, IDX_DIM + h:IDX_DIM + h + 1] * index_scale
        wb_ref[h] = jnp.broadcast_to(w_h, (qb, LANES))

    q_pos = i * qb + lax.broadcasted_iota(I32, (qb, sc), 0)

    def score_chunk(c, carry):
        ks = ki_ref[pl.ds(pl.multiple_of(c * sc, sc), sc), :]
        parts = [jnp.zeros((qb, LANES), F32) for _ in range(sc // LANES)]
        for h in range(N_IDX_HEADS):
            logit = _dot_nt(qh_ref[h], ks)
            w_b = wb_ref[h]
            for j in range(sc // LANES):
                parts[j] = parts[j] + jnp.maximum(logit[:, j * LANES:(j + 1) * LANES], 0.0) * w_b
        score = jnp.concatenate(parts, axis=-1)
        s_pos = c * sc + lax.broadcasted_iota(I32, (qb, sc), 1)
        keys_ref[c] = jnp.where(s_pos <= q_pos, _sort_key(score), NEG_INF_KEY)
        return carry

    lax.fori_loop(0, n_chunks, score_chunk, 0)

    def fold(f, init):
        return lax.fori_loop(0, n_chunks, lambda c, carry: f(carry, keys_ref[c], c * sc), init)

    def update(f):
        def body(c, carry):
            keys_ref[c] = f(keys_ref[c], c * sc)
            return carry
        lax.fori_loop(0, n_chunks, body, 0)

    thr = _topk_threshold(fold, update, qb, k_top, n_pos_bits, stage_ref)
    thr_b = jnp.broadcast_to(thr, (qb, LANES))

    units = [(kv, g0) for kv in range(N_KV_HEADS) for g0 in range(0, N_GROUP, HEADS_PER_UNIT)]
    m_ref[...] = jnp.full(m_ref.shape, MASK_BIAS, F32)
    l_ref[...] = jnp.zeros(l_ref.shape, F32)
    acc_ref[...] = jnp.zeros(acc_ref.shape, F32)

    def attn_chunk(c, carry):
        keys = keys_ref[c]
        bias = jnp.concatenate(
            [jnp.where(keys[:, j * LANES:(j + 1) * LANES] >= thr_b, 0.0, MASK_BIAS)
             for j in range(sc // LANES)], axis=-1)
        bias = jnp.concatenate([bias] * HEADS_PER_UNIT, axis=0)
        start = pl.multiple_of(c * sc, sc)
        for u, (kv, g0) in enumerate(units):
            qc = jnp.concatenate(
                [q_ref[:, (kv * N_GROUP + g) * HEAD_DIM:(kv * N_GROUP + g + 1) * HEAD_DIM]
                 for g in range(g0, g0 + HEADS_PER_UNIT)], axis=0)
            kc = k_ref[pl.ds(start, sc), kv * HEAD_DIM:(kv + 1) * HEAD_DIM]
            vc = v_ref[pl.ds(start, sc), kv * HEAD_DIM:(kv + 1) * HEAD_DIM]
            _softmax_step(_dot_nt(qc, kc) + bias, m_ref, l_ref, acc_ref, vc, u)
        return carry

    lax.fori_loop(0, n_chunks, attn_chunk, 0)

    for u, (kv, g0) in enumerate(units):
        o = acc_ref[u] / l_ref[u]
        for j in range(HEADS_PER_UNIT):
            hd = kv * N_GROUP + g0 + j
            o_ref[:, hd * HEAD_DIM:(hd + 1) * HEAD_DIM] = o[j * qb:(j + 1) * qb, :].astype(o_ref.dtype)


def _attn_prompt(qi, kw, q, ki2, kb, vb, qb, sc):
    B, S, _ = q.shape
    k_top = min(TOPK_MAX, S // 4)
    assert sc >= k_top and S % sc == 0 and S % qb == 0 and sc % qb == 0
    qblk = lambda n: pl.BlockSpec((None, qb, n), lambda b, i: (b, i, 0))
    whole = lambda n: pl.BlockSpec((None, S, n), lambda b, i: (b, 0, 0), pipeline_mode=pl.Buffered(1))
    rows = HEADS_PER_UNIT * qb
    n_units = N_HEADS // HEADS_PER_UNIT
    return pl.pallas_call(
        functools.partial(_attn_prompt_kernel, qb=qb, sc=sc, k_top=k_top,
                          index_scale=(N_IDX_HEADS * IDX_DIM) ** -0.5),
        grid=(B, S // qb),
        in_specs=[qblk(qi.shape[2]), qblk(kw.shape[2]), qblk(q.shape[2]),
                  whole(ki2.shape[2]), whole(kb.shape[2]), whole(vb.shape[2])],
        out_specs=qblk(q.shape[2]),
        out_shape=jax.ShapeDtypeStruct(q.shape, BF16),
        scratch_shapes=[
            pltpu.VMEM((S // sc, qb, sc), I32),
            pltpu.VMEM((N_IDX_HEADS, qb, LANES), BF16),
            pltpu.VMEM((N_IDX_HEADS, qb, LANES), F32),
            pltpu.VMEM((3, qb, LANES), I32),
            pltpu.VMEM((n_units, rows, LANES), F32),
            pltpu.VMEM((n_units, rows, LANES), F32),
            pltpu.VMEM((n_units, rows, HEAD_DIM), F32),
        ],
        compiler_params=_params("parallel", "arbitrary"),
        name="attn_prompt",
    )(qi, kw, q, ki2, kb, vb)


def _attn_sample_kernel(pt_ref, qi_ref, kw_ref, q_ref, kin_ref, kn_ref, vn_ref,
                        ci_hbm, ck_hbm, cv_hbm, o_ref,
                        keys_ref, keyn_ref, ibuf, kbuf, vbuf, m_ref, l_ref, acc_ref,
                        ipg, kpg, vpg, sem_i, sem_k, sem_v,
                        *, n_pg, n_steps, t_new, page, k_top, index_scale):
    b = pl.program_id(0)
    sc = n_pg * page

    def group_copies(src, dst, sem, step, slot):
        return [pltpu.make_async_copy(src.at[pt_ref[b, step * n_pg + p]], dst.at[slot, p], sem.at[slot])
                for p in range(n_pg)]

    def start_group(src, dst, sem, step, slot):
        for cp in group_copies(src, dst, sem, step, slot):
            cp.start()

    def wait_group(src, dst, sem, step, slot):
        for cp in group_copies(src, dst, sem, step, slot):
            cp.wait()

    start_group(ci_hbm, ipg, sem_i, 0, 0)
    kv_slots = kpg.shape[0]
    for s0 in range(min(kv_slots - 1, n_steps)):
        start_group(ck_hbm, kpg, sem_k, s0, s0)
        start_group(cv_hbm, vpg, sem_v, s0, s0)

    def head_sum(x):
        return jnp.sum(x.reshape(N_IDX_HEADS, t_new, x.shape[-1]), axis=0)

    qi = qi_ref[...].astype(F32)
    lane = lax.broadcasted_iota(I32, (t_new, LANES), 1)
    pieces, wcols = [], []
    for h in range(N_IDX_HEADS):
        blk = qi[:, (h // 2) * LANES:(h // 2 + 1) * LANES]
        keep = (lane < IDX_DIM) if h % 2 == 0 else (lane >= IDX_DIM)
        pieces.append(jnp.where(keep, blk, 0.0))
        w_h = kw_ref[:, IDX_DIM + h:IDX_DIM + h + 1] * index_scale
        wcols.append(jnp.broadcast_to(w_h, (t_new, LANES)))
    lhs = jnp.concatenate(pieces, axis=0).astype(BF16)
    w_b = jnp.concatenate(wcols, axis=0)

    def scores(key_mat):
        logit = _dot(lhs, key_mat)
        parts = [head_sum(jnp.maximum(logit[:, c * LANES:(c + 1) * LANES], 0.0) * w_b)
                 for c in range(key_mat.shape[1] // LANES)]
        return jnp.concatenate(parts, axis=-1)

    def score_step(s, carry):
        slot = s % 2

        @pl.when(s + 1 < n_steps)
        def _():
            start_group(ci_hbm, ipg, sem_i, s + 1, 1 - slot)

        wait_group(ci_hbm, ipg, sem_i, s, slot)
        for p in range(n_pg):
            pg = ipg[slot, p].astype(BF16)
            ibuf[:, p * page:(p + 1) * page] = jnp.concatenate([pg, pg], axis=0)
        keys_ref[s] = _sort_key(scores(ibuf[...]))
        return carry

    lax.fori_loop(0, n_steps, score_step, 0)

    s_new = scores(kin_ref[...])
    t_pos = lax.broadcasted_iota(I32, (t_new, LANES), 0)
    s_pos = lax.broadcasted_iota(I32, (t_new, LANES), 1)
    keyn_ref[...] = jnp.where(s_pos <= t_pos, _sort_key(s_new), NEG_INF_KEY)

    past = n_steps * sc

    def fold(f, init):
        carry = init
        for c in range(n_steps):
            carry = f(carry, keys_ref[c], c * sc)
        return f(carry, keyn_ref[...], past)

    def update(f):
        for c in range(n_steps):
            keys_ref[c] = f(keys_ref[c], c * sc)
        keyn_ref[...] = f(keyn_ref[...], past)

    thr = _topk_threshold(fold, update, t_new, k_top, (past + LANES).bit_length())
    thr_b = jnp.broadcast_to(thr, (t_new, LANES))
    m_ref[...] = jnp.full(m_ref.shape, MASK_BIAS, F32)
    l_ref[...] = jnp.zeros(l_ref.shape, F32)
    acc_ref[...] = jnp.zeros(acc_ref.shape, F32)

    def attend(keys, k_of, v_of):
        bias = jnp.concatenate(
            [jnp.where(keys[:, c * LANES:(c + 1) * LANES] >= thr_b, 0.0, MASK_BIAS)
             for c in range(keys.shape[1] // LANES)], axis=-1)
        bias = jnp.concatenate([bias] * N_GROUP, axis=0)
        q = q_ref[...].astype(F32)
        for kv in range(N_KV_HEADS):
            qc = jnp.concatenate(
                [q[:, (kv * N_GROUP + g) * HEAD_DIM:(kv * N_GROUP + g + 1) * HEAD_DIM]
                 for g in range(N_GROUP)], axis=0).astype(BF16)
            _softmax_step(_dot_nt(qc, k_of(kv)) + bias, m_ref, l_ref, acc_ref, v_of(kv), kv)

    def attend_step(s, carry):
        slot = s % kv_slots
        ahead = s + kv_slots - 1

        @pl.when(ahead < n_steps)
        def _():
            start_group(ck_hbm, kpg, sem_k, ahead, ahead % kv_slots)
            start_group(cv_hbm, vpg, sem_v, ahead, ahead % kv_slots)

        wait_group(ck_hbm, kpg, sem_k, s, slot)
        wait_group(cv_hbm, vpg, sem_v, s, slot)
        for p in range(n_pg):
            for kv in range(N_KV_HEADS):
                rows_kv = pl.ds(kv, page, stride=N_KV_HEADS)
                kbuf[kv, p * page:(p + 1) * page, :] = kpg[slot, p, rows_kv, :].astype(BF16)
                vbuf[kv, p * page:(p + 1) * page, :] = vpg[slot, p, rows_kv, :].astype(BF16)
        attend(keys_ref[s], lambda kv: kbuf[kv], lambda kv: vbuf[kv])
        return carry

    lax.fori_loop(0, n_steps, attend_step, 0)

    attend(keyn_ref[...],
           lambda kv: kn_ref[:, kv * HEAD_DIM:(kv + 1) * HEAD_DIM],
           lambda kv: vn_ref[:, kv * HEAD_DIM:(kv + 1) * HEAD_DIM])
    for kv in range(N_KV_HEADS):
        o = acc_ref[kv] / l_ref[kv]
        for g in range(N_GROUP):
            hd = kv * N_GROUP + g
            o_ref[:, hd * HEAD_DIM:(hd + 1) * HEAD_DIM] = (
                o[g * t_new:(g + 1) * t_new, :].astype(o_ref.dtype))


def _attn_sample(page_table, qi, kw, q, ki2n_t, kbn, vbn, cache_ik_t, cache_k, cache_v, n_pg):
    B, T, _ = q.shape
    n_pages = page_table.shape[1]
    page = cache_ik_t.shape[2]
    assert n_pages % n_pg == 0 and page == LANES and cache_k.shape[1] == page * N_KV_HEADS
    assert cache_ik_t.shape[1] == IDX_DIM
    n_steps = n_pages // n_pg
    kv_slots = min(KV_SLOTS, n_steps)
    sc = n_pg * page
    k_top = min(TOPK_MAX, (n_pages * page + T) // 4)
    rows = N_GROUP * T

    per_b = lambda a: pl.BlockSpec((None,) + a.shape[1:], lambda b, pt: (b, 0, 0))
    hbm = pl.BlockSpec(memory_space=pl.ANY)
    grid_spec = pltpu.PrefetchScalarGridSpec(
        num_scalar_prefetch=1,
        grid=(B,),
        in_specs=[per_b(qi), per_b(kw), per_b(q), per_b(ki2n_t), per_b(kbn), per_b(vbn), hbm, hbm, hbm],
        out_specs=pl.BlockSpec((None, T, q.shape[2]), lambda b, pt: (b, 0, 0)),
        scratch_shapes=[
            pltpu.VMEM((n_steps, T, sc), I32),
            pltpu.VMEM((T, LANES), I32),
            pltpu.VMEM((2 * IDX_DIM, sc), BF16),
            pltpu.VMEM((N_KV_HEADS, sc, HEAD_DIM), BF16),
            pltpu.VMEM((N_KV_HEADS, sc, HEAD_DIM), BF16),
            pltpu.VMEM((N_KV_HEADS, rows, LANES), F32),
            pltpu.VMEM((N_KV_HEADS, rows, LANES), F32),
            pltpu.VMEM((N_KV_HEADS, rows, HEAD_DIM), F32),
            pltpu.VMEM((2, n_pg) + cache_ik_t.shape[1:], cache_ik_t.dtype),
            pltpu.VMEM((kv_slots, n_pg) + cache_k.shape[1:], cache_k.dtype),
            pltpu.VMEM((kv_slots, n_pg) + cache_v.shape[1:], cache_v.dtype),
            pltpu.SemaphoreType.DMA((2,)),
            pltpu.SemaphoreType.DMA((kv_slots,)),
            pltpu.SemaphoreType.DMA((kv_slots,)),
        ],
    )
    return pl.pallas_call(
        functools.partial(_attn_sample_kernel, n_pg=n_pg, n_steps=n_steps, t_new=T, page=page,
                          k_top=k_top, index_scale=(N_IDX_HEADS * IDX_DIM) ** -0.5),
        grid_spec=grid_spec,
        out_shape=jax.ShapeDtypeStruct(q.shape, BF16),
        compiler_params=_params("arbitrary"),
        name="attn_sample",
    )(page_table, qi, kw, q, ki2n_t, kbn, vbn, cache_ik_t, cache_k, cache_v)


def _mix_kernel(x_ref, c_ref, a_ref, w_ref, g1_ref, g2_ref, x1_ref, h_ref, *, c_conv):
    tm = x_ref.shape[0]
    for r0 in range(0, tm, tm // 2):
        rs = slice(r0, r0 + tm // 2)
        mixed = _dot(c_ref[rs, :], w_ref[0:c_conv, :]) + _dot(a_ref[rs, :], w_ref[c_conv:, :])
        x1 = x_ref[rs, :] + _rms(mixed, g1_ref[...])
        x1_ref[rs, :] = x1
        h_ref[rs, :] = _rms(x1, g2_ref[...]).astype(BF16)


def _mix(x, conv_o, attn_o, w_out, g_post, g_pre, tm):
    T, D = x.shape
    row = lambda n: pl.BlockSpec((tm, n), lambda i: (i, 0))
    return pl.pallas_call(
        functools.partial(_mix_kernel, c_conv=conv_o.shape[1]),
        grid=(T // tm,),
        in_specs=[row(D), row(conv_o.shape[1]), row(attn_o.shape[1]),
                  _resident(w_out.shape), _resident((1, D)), _resident((1, D))],
        out_specs=(row(D), row(D)),
        out_shape=(jax.ShapeDtypeStruct((T, D), F32), jax.ShapeDtypeStruct((T, D), BF16)),
        compiler_params=_params("parallel"),
        name="mix",
    )(x, conv_o, attn_o, w_out, g_post, g_pre)


def _ffn_kernel(x_ref, h_ref, wg_ref, wu_ref, wd_ref, g_ref, y_ref, acc_ref):
    j = pl.program_id(1)

    @pl.when(j == 0)
    def _():
        acc_ref[...] = jnp.zeros(acc_ref.shape, F32)

    h = h_ref[...]
    gate = _dot(h, wg_ref[...])
    a = (gate * _sigmoid(gate) * _dot(h, wu_ref[...])).astype(BF16)
    acc_ref[...] += _dot(a, wd_ref[...])

    @pl.when(j == pl.num_programs(1) - 1)
    def _():
        y_ref[...] = x_ref[...] + _rms(acc_ref[...], g_ref[...])


def _ffn(x1, h, w_gate, w_up, w_down, g_post, tm, tf):
    T, D = x1.shape
    FF = w_gate.shape[1]
    assert FF % tf == 0
    row = lambda: pl.BlockSpec((tm, D), lambda i, j: (i, 0))
    return pl.pallas_call(
        _ffn_kernel,
        grid=(T // tm, FF // tf),
        in_specs=[row(), row(),
                  pl.BlockSpec((D, tf), lambda i, j: (0, j)),
                  pl.BlockSpec((D, tf), lambda i, j: (0, j)),
                  pl.BlockSpec((tf, D), lambda i, j: (j, 0)),
                  _resident((1, D))],
        out_specs=row(),
        out_shape=jax.ShapeDtypeStruct((T, D), F32),
        scratch_shapes=[pltpu.VMEM((tm, D), F32)],
        compiler_params=_params("parallel", "arbitrary"),
        name="ffn",
    )(x1, h, w_gate, w_up, w_down, g_post)


def _tile(n, pref):
    t = min(pref, n)
    while n % t:
        t //= 2
    return t


def _ffn_tile(ff):
    for t in (512, 256, 128):
        if ff % t == 0:
            return t
    return ff


def _dup_lanes(ki):
    return jnp.concatenate([ki, ki], axis=-1).astype(BF16)


def kernel(x_prompt, x_sample, cache_k, cache_v, cache_idx_k, state_conv, page_table,
           g_pre_mix, w_in, conv_w, conv_b, conv_ln_g, conv_ln_b, w_out,
           g_post_mix, g_pre_ffn, w_gate, w_up, w_down, g_post_ffn):
    B, S, D = x_prompt.shape
    Bd, Td, _ = x_sample.shape
    depth = w_in.shape[0]
    c_conv = conv_w.shape[2]
    c_q = N_HEADS * HEAD_DIM
    c_kv = N_KV_HEADS * HEAD_DIM
    c_qi = N_IDX_HEADS * IDX_DIM
    sizes = (c_conv, c_conv, c_q, c_kv, c_kv, c_qi, IDX_DIM, N_IDX_HEADS)
    n_in = sum(sizes)
    assert w_in.shape[2] == n_in
    n_pad = -n_in % LANES
    n_pool, page = cache_k.shape[1], cache_k.shape[2]

    xp = x_prompt.reshape(B * S, D)
    xs = x_sample.reshape(Bd * Td, D)
    outs = [[] for _ in range(8)]
    row2 = lambda a: a.reshape(1, -1)

    for l in range(depth):
        w_in_b = jnp.pad(w_in[l], ((0, 0), (0, n_pad))).astype(BF16)
        w_out_b = w_out[l].astype(BF16)
        wg_b, wu_b, wd_b = w_gate[l].astype(BF16), w_up[l].astype(BF16), w_down[l].astype(BF16)
        g_pre, g_post, g_ffn_pre, g_ffn_post = (row2(g_pre_mix[l]), row2(g_post_mix[l]),
                                                row2(g_pre_ffn[l]), row2(g_post_ffn[l]))
        cw, cb, lg, lb = conv_w[l], row2(conv_b[l]), row2(conv_ln_g[l]), row2(conv_ln_b[l])
        tf = _ffn_tile(wg_b.shape[1])

        tm = _tile(B * S, 512)
        u, q, k, v, kb, vb, qi, kw = _proj(xp, g_pre, w_in_b, sizes, tm)
        r3 = lambda a: a.reshape(B, S, a.shape[-1])
        conv_o = _conv_prompt(r3(u), cw, cb, lg, lb, _tile(S, 256))
        ki = kw[:, :IDX_DIM]
        attn_o = _attn_prompt(r3(qi), r3(kw), r3(q), r3(_dup_lanes(ki)), r3(kb), r3(vb),
                              qb=_tile(S, 512), sc=_tile(S, 512))
        x1, h2 = _mix(xp, conv_o.reshape(B * S, c_conv), attn_o.reshape(B * S, c_q),
                      w_out_b, g_post, g_ffn_pre, tm)
        xp = _ffn(x1, h2, wg_b, wu_b, wd_b, g_ffn_post, tm, tf)
        outs[0].append(k.reshape(B, S, N_KV_HEADS, HEAD_DIM))
        outs[1].append(v.reshape(B, S, N_KV_HEADS, HEAD_DIM))
        outs[2].append(ki.reshape(B, S, IDX_DIM))
        outs[3].append(r3(u)[:, S - (cw.shape[0] - 1):, :])

        tms = _tile(Bd * Td, 256)
        u, q, k, v, kb, vb, qi, kw = _proj(xs, g_pre, w_in_b, sizes, tms)
        r3 = lambda a: a.reshape(Bd, Td, a.shape[-1])
        conv_o, conv_st = _conv_sample(r3(u), state_conv[l], cw, cb, lg, lb)
        ki = kw[:, :IDX_DIM]
        pad_rows = lambda a: jnp.pad(r3(a), ((0, 0), (0, LANES - Td), (0, 0)))
        ki_t = jnp.pad(jnp.swapaxes(r3(ki), 1, 2), ((0, 0), (0, 0), (0, LANES - Td))).astype(BF16)
        attn_o = _attn_sample(
            page_table, r3(qi), r3(kw), r3(q), jnp.concatenate([ki_t, ki_t], axis=1),
            pad_rows(kb), pad_rows(vb),
            jnp.swapaxes(cache_idx_k[l], 1, 2),

            cache_k[l].reshape(n_pool, page * N_KV_HEADS, HEAD_DIM),
            cache_v[l].reshape(n_pool, page * N_KV_HEADS, HEAD_DIM),
            n_pg=_tile(page_table.shape[1], 16))
        x1, h2 = _mix(xs, conv_o.reshape(Bd * Td, c_conv), attn_o.reshape(Bd * Td, c_q),
                      w_out_b, g_post, g_ffn_pre, tms)
        xs = _ffn(x1, h2, wg_b, wu_b, wd_b, g_ffn_post, tms, tf)
        outs[4].append(k.reshape(Bd, Td, N_KV_HEADS, HEAD_DIM))
        outs[5].append(v.reshape(Bd, Td, N_KV_HEADS, HEAD_DIM))
        outs[6].append(ki.reshape(Bd, Td, IDX_DIM))
        outs[7].append(conv_st)

    return (xp.reshape(B, S, D), xs.reshape(Bd, Td, D), *[jnp.stack(o) for o in outs])
```

```python
import functools

import jax
import jax.numpy as jnp
from jax import lax
from jax.experimental import pallas as pl
from jax.experimental.pallas import tpu as pltpu

N_HEADS = 8
N_KV_HEADS = 2
N_GROUP = N_HEADS // N_KV_HEADS
HEAD_DIM = 128
N_IDX_HEADS = 16
IDX_DIM = 64
TOPK_MAX = 256
EPS = 1e-6
LANES = 128
SUBLANES = 8
VMEM_LIMIT = 56 * 1024 * 1024

INT_MIN = -2 ** 31
NEG_INF_KEY = (0xFF800000 ^ 0x7FFFFFFF) - 2 ** 32
MASK_BIAS = -1e30
COUNT_ROWS = 64
KV_SLOTS = 8
HEADS_PER_UNIT = 4
LOG2_E = 1.4426950408889634

F32 = jnp.float32
BF16 = jnp.bfloat16
I32 = jnp.int32


def _dot(a, b):
    return jnp.dot(a, b, preferred_element_type=F32)


def _dot_nt(a, b):
    return lax.dot_general(a, b, (((1,), (1,)), ((), ())), preferred_element_type=F32)


def _sigmoid(x):
    return 1.0 / (1.0 + jnp.exp(-x))


def _rms(x, g):
    return x * lax.rsqrt(jnp.mean(x * x, axis=-1, keepdims=True) + EPS) * g


def _sort_key(x):
    b = pltpu.bitcast(x, I32)
    return b ^ ((b >> 31) & 0x7FFFFFFF)


def _params(*sem):
    return pltpu.CompilerParams(dimension_semantics=sem, vmem_limit_bytes=VMEM_LIMIT)


def _resident(shape):
    nd = len(shape)
    return pl.BlockSpec(shape, lambda *_: (0,) * nd, pipeline_mode=pl.Buffered(1))


def _proj_kernel(x_ref, g_ref, w_ref, u_ref, q_ref, k_ref, v_ref, kb_ref, vb_ref,
                 qi_ref, kw_ref, *, offs, attn_scale):
    h = _rms(x_ref[...], g_ref[...]).astype(BF16)
    o_ua, o_ug, o_q, o_k, o_v, o_qi, o_kw, o_end = offs
    ua = _dot(h, w_ref[:, o_ua:o_ug])
    ug = _dot(h, w_ref[:, o_ug:o_q])
    u_ref[...] = ua * _sigmoid(ug)
    q_ref[...] = (_dot(h, w_ref[:, o_q:o_k]) * attn_scale).astype(BF16)
    k = _dot(h, w_ref[:, o_k:o_v])
    k_ref[...] = k
    kb_ref[...] = k.astype(BF16)
    v = _dot(h, w_ref[:, o_v:o_qi])
    v_ref[...] = v
    vb_ref[...] = v.astype(BF16)
    qi_ref[...] = _dot(h, w_ref[:, o_qi:o_kw]).astype(BF16)
    kw_ref[...] = _dot(h, w_ref[:, o_kw:o_end])


def _proj(x, g, w, sizes, tm):
    T, D = x.shape
    c_conv, _, c_q, c_kv, _, c_qi, c_ki, c_wi = sizes
    offs = [0]
    for s in sizes[:6]:
        offs.append(offs[-1] + s)
    offs.append(w.shape[1])
    kw_w = offs[-1] - offs[-2]
    row = lambda n: pl.BlockSpec((tm, n), lambda i: (i, 0))
    out_shape = (
        jax.ShapeDtypeStruct((T, c_conv), F32),
        jax.ShapeDtypeStruct((T, c_q), BF16),
        jax.ShapeDtypeStruct((T, c_kv), F32),
        jax.ShapeDtypeStruct((T, c_kv), F32),
        jax.ShapeDtypeStruct((T, c_kv), BF16),
        jax.ShapeDtypeStruct((T, c_kv), BF16),
        jax.ShapeDtypeStruct((T, c_qi), BF16),
        jax.ShapeDtypeStruct((T, kw_w), F32),
    )
    return pl.pallas_call(
        functools.partial(_proj_kernel, offs=tuple(offs), attn_scale=HEAD_DIM ** -0.5 * LOG2_E),
        grid=(T // tm,),
        in_specs=[row(D), _resident((1, D)), _resident(w.shape)],
        out_specs=tuple(row(s.shape[1]) for s in out_shape),
        out_shape=out_shape,
        compiler_params=_params("parallel"),
        name="proj",
    )(x, g, w)


HIST_PAD = 32


def _conv_body(full_ref, sh_ref, y_ref, w_ref, b_ref, g_ref, beta_ref, o_ref, *, width, tb, cw):
    C = full_ref.shape[1]
    base = HIST_PAD - (width - 1)
    n_b = min(SUBLANES, width)
    n_a = [(width - 1 - b) // SUBLANES + 1 for b in range(n_b)]
    for b in range(n_b):
        rows_b = tb + SUBLANES * (n_a[b] - 1)
        sh_ref[b, 0:rows_b, :] = full_ref[base + b:base + b + rows_b, :]
    for c in range(C // cw):
        cs = slice(c * cw, (c + 1) * cw)
        acc = jnp.zeros((tb, cw), F32)
        for b in range(n_b):
            for a in range(n_a[b]):
                j = b + SUBLANES * a
                acc = acc + sh_ref[b, SUBLANES * a:SUBLANES * a + tb, cs] * w_ref[j:j + 1, cs]
        y_ref[:, cs] = acc + b_ref[:, cs]
    y = y_ref[...]
    mu = jnp.mean(y, axis=-1, keepdims=True)
    yc = y - mu
    z = yc * lax.rsqrt(jnp.mean(yc * yc, axis=-1, keepdims=True) + EPS) * g_ref[...] + beta_ref[...]
    o_ref[...] = (z * _sigmoid(z)).astype(o_ref.dtype)


def _conv_scratch(tb, C, width):
    n_b = min(SUBLANES, width)
    return [pltpu.VMEM((HIST_PAD + tb, C), F32),
            pltpu.VMEM((n_b, tb + SUBLANES * ((width - 1) // SUBLANES), C), F32),
            pltpu.VMEM((tb, C), F32)]


def _conv_prompt_kernel(prev_ref, u_ref, w_ref, b_ref, g_ref, beta_ref, o_ref,
                        full_ref, sh_ref, y_ref, *, width, tb, cw):
    @pl.when(pl.program_id(1) == 0)
    def _():
        full_ref[0:HIST_PAD, :] = jnp.zeros((HIST_PAD, full_ref.shape[1]), F32)

    @pl.when(pl.program_id(1) > 0)
    def _():
        full_ref[0:HIST_PAD, :] = prev_ref[...]

    full_ref[HIST_PAD:HIST_PAD + tb, :] = u_ref[...]
    _conv_body(full_ref, sh_ref, y_ref, w_ref, b_ref, g_ref, beta_ref, o_ref, width=width, tb=tb, cw=cw)


def _conv_prompt(u, conv_w, conv_b, ln_g, ln_b, tb):
    B, S, C = u.shape
    width = conv_w.shape[0]
    r = tb // HIST_PAD
    vec = lambda: _resident((1, C))
    return pl.pallas_call(
        functools.partial(_conv_prompt_kernel, width=width, tb=tb, cw=128),
        grid=(B, S // tb),
        in_specs=[
            pl.BlockSpec((None, HIST_PAD, C), lambda b, i: (b, jnp.maximum(i * r - 1, 0), 0)),
            pl.BlockSpec((None, tb, C), lambda b, i: (b, i, 0)),
            _resident(conv_w.shape), vec(), vec(), vec(),
        ],
        out_specs=pl.BlockSpec((None, tb, C), lambda b, i: (b, i, 0)),
        out_shape=jax.ShapeDtypeStruct((B, S, C), BF16),
        scratch_shapes=_conv_scratch(tb, C, width),
        compiler_params=_params("parallel", "arbitrary"),
        name="conv_prompt",
    )(u, u, conv_w, conv_b, ln_g, ln_b)


def _conv_sample_kernel(hist_ref, u_ref, w_ref, b_ref, g_ref, beta_ref, o_ref, st_ref,
                        full_ref, sh_ref, y_ref, *, width, tb, cw):
    full_ref[0:HIST_PAD, :] = hist_ref[...]
    full_ref[HIST_PAD:HIST_PAD + tb, :] = u_ref[...]
    _conv_body(full_ref, sh_ref, y_ref, w_ref, b_ref, g_ref, beta_ref, o_ref, width=width, tb=tb, cw=cw)
    st_ref[...] = full_ref[HIST_PAD + tb - (width - 1):HIST_PAD + tb, :]


def _conv_sample(u, hist, conv_w, conv_b, ln_g, ln_b):
    B, T, C = u.shape
    width = conv_w.shape[0]
    hist_p = jnp.pad(hist, ((0, 0), (HIST_PAD - (width - 1), 0), (0, 0)))
    vec = lambda: _resident((1, C))
    return pl.pallas_call(
        functools.partial(_conv_sample_kernel, width=width, tb=T, cw=256),
        grid=(B,),
        in_specs=[
            pl.BlockSpec((None, HIST_PAD, C), lambda b: (b, 0, 0)),
            pl.BlockSpec((None, T, C), lambda b: (b, 0, 0)),
            _resident(conv_w.shape), vec(), vec(), vec(),
        ],
        out_specs=(pl.BlockSpec((None, T, C), lambda b: (b, 0, 0)),
                   pl.BlockSpec((None, width - 1, C), lambda b: (b, 0, 0))),
        out_shape=(jax.ShapeDtypeStruct((B, T, C), BF16),
                   jax.ShapeDtypeStruct((B, width - 1, C), F32)),
        scratch_shapes=_conv_scratch(T, C, width),
        compiler_params=_params("parallel"),
        name="conv_sample",
    )(hist_p, u, conv_w, conv_b, ln_g, ln_b)


def _count(fold, rows, pred, row_vals, stage_ref):
    grouped = rows > COUNT_ROWS
    if grouped:
        for n, v in enumerate(row_vals):
            stage_ref[n] = jnp.broadcast_to(v, (rows, LANES))

    def group(cnt, keys, pos0, vals):
        hits = [jnp.where(pred(keys[:, j * LANES:(j + 1) * LANES], pos0 + j * LANES, *vals), 1, 0)
                for j in range(keys.shape[1] // LANES)]
        while len(hits) > 1:
            hits = [a + b for a, b in zip(hits[::2], hits[1::2])] + hits[len(hits) & ~1:]
        return cnt + hits[0]

    def add_block(cnt, keys, pos0):
        if not grouped:
            return group(cnt, keys, pos0, [jnp.broadcast_to(v, (rows, LANES)) for v in row_vals])
        return jnp.concatenate(
            [group(cnt[r:r + COUNT_ROWS], keys[r:r + COUNT_ROWS], pos0,
                   [stage_ref[n, r:r + COUNT_ROWS, :] for n in range(len(row_vals))])
             for r in range(0, rows, COUNT_ROWS)], axis=0)

    cnt = fold(add_block, jnp.zeros((rows, LANES), I32))
    return jnp.sum(cnt.astype(F32), axis=-1, keepdims=True).astype(I32)


def _topk_threshold(fold, update, rows, k_top, n_pos_bits, stage_ref=None):
    def bcast(x):
        return jnp.broadcast_to(x, (rows, LANES))

    def count_ge(cand):
        return _count(fold, rows, lambda keys, pos0, c: keys >= c, [cand], stage_ref)

    def bit_step(it, state):
        ub, c_ub = state
        cand_u = ub | jnp.left_shift(jnp.int32(1), 31 - it)
        cnt = count_ge(cand_u ^ INT_MIN)
        keep = cnt >= k_top
        return jnp.where(keep, cand_u, ub), jnp.where(keep, cnt, c_ub)

    start = (jnp.zeros((rows, 1), I32), jnp.full((rows, 1), k_top + 1, I32))
    ub, c_thr = lax.fori_loop(0, 32, bit_step, start)
    thr = ub ^ INT_MIN
    tie = (c_thr > k_top) & (thr > NEG_INF_KEY)

    @pl.when(jnp.sum(jnp.where(tie, 1.0, 0.0)) > 0.0)
    def _():
        need = k_top - count_ge(thr + 1)
        thr_b = bcast(thr)
        lane = lax.broadcasted_iota(I32, (rows, LANES), 1)

        def tied_before(cut):
            return _count(fold, rows, lambda keys, pos0, t, c, ln: (keys == t) & (ln + pos0 < c),
                          [thr, cut, lane], stage_ref)

        def cut_step(it, cut):
            cand = cut | jnp.left_shift(jnp.int32(1), n_pos_bits - 1 - it)
            return jnp.where(tied_before(cand) <= need, cand, cut)

        cut = lax.fori_loop(0, n_pos_bits, cut_step, jnp.zeros((rows, 1), I32))
        cut_b = bcast(jnp.where(tie, cut, 2 ** n_pos_bits))

        def drop(keys, pos0):
            parts = []
            for j in range(keys.shape[1] // LANES):
                blk = keys[:, j * LANES:(j + 1) * LANES]
                surplus = (blk == thr_b) & (lane + (pos0 + j * LANES) >= cut_b)
                parts.append(jnp.where(surplus, NEG_INF_KEY, blk))
            return jnp.concatenate(parts, axis=-1)

        update(drop)

    return jnp.maximum(thr, NEG_INF_KEY + 1)


def _softmax_step(s, m_ref, l_ref, acc_ref, v, c):
    m_old = m_ref[c]
    m_new = jnp.maximum(m_old, jnp.max(s, axis=-1, keepdims=True))
    alpha = jnp.exp2(m_old - m_new)
    p = jnp.exp2(s - jnp.tile(m_new, (1, s.shape[1] // LANES)))
    l_ref[c] = alpha * l_ref[c] + jnp.sum(p, axis=-1, keepdims=True)
    acc_ref[c] = alpha * acc_ref[c] + _dot(p.astype(BF16), v)
    m_ref[c] = m_new


def _attn_prompt_kernel(qi_ref, kw_ref, q_ref, ki_ref, k_ref, v_ref, o_ref,
                        keys_ref, qh_ref, wb_ref, stage_ref, m_ref, l_ref, acc_ref,
                        *, qb, sc, k_top, index_scale):
    i = pl.program_id(1)
    n_pos_bits = (keys_ref.shape[0] * sc).bit_length()
    n_chunks = ((i + 1) * qb + sc - 1) // sc

    lane = lax.broadcasted_iota(I32, (qb, LANES), 1)
    for h in range(N_IDX_HEADS):
        blk = qi_ref[:, (h // 2) * LANES:(h // 2 + 1) * LANES]
        keep = (lane < IDX_DIM) if h % 2 == 0 else (lane >= IDX_DIM)
        qh_ref[h] = jnp.where(keep, blk, jnp.zeros_like(blk))
        w_h = kw_ref[:, IDX_DIM + h:IDX_DIM + h + 1] * index_scale
        wb_ref[h] = jnp.broadcast_to(w_h, (qb, LANES))

    q_pos = i * qb + lax.broadcasted_iota(I32, (qb, sc), 0)

    def score_chunk(c, carry):
        ks = ki_ref[pl.ds(pl.multiple_of(c * sc, sc), sc), :]
        parts = [jnp.zeros((qb, LANES), F32) for _ in range(sc // LANES)]
        for h in range(N_IDX_HEADS):
            logit = _dot_nt(qh_ref[h], ks)
            w_b = wb_ref[h]
            for j in range(sc // LANES):
                parts[j] = parts[j] + jnp.maximum(logit[:, j * LANES:(j + 1) * LANES], 0.0) * w_b
        score = jnp.concatenate(parts, axis=-1)
        s_pos = c * sc + lax.broadcasted_iota(I32, (qb, sc), 1)
        keys_ref[c] = jnp.where(s_pos <= q_pos, _sort_key(score), NEG_INF_KEY)
        return carry

    lax.fori_loop(0, n_chunks, score_chunk, 0)

    def fold(f, init):
        return lax.fori_loop(0, n_chunks, lambda c, carry: f(carry, keys_ref[c], c * sc), init)

    def update(f):
        def body(c, carry):
            keys_ref[c] = f(keys_ref[c], c * sc)
            return carry
        lax.fori_loop(0, n_chunks, body, 0)

    thr = _topk_threshold(fold, update, qb, k_top, n_pos_bits, stage_ref)
    thr_b = jnp.broadcast_to(thr, (qb, LANES))

    units = [(kv, g0) for kv in range(N_KV_HEADS) for g0 in range(0, N_GROUP, HEADS_PER_UNIT)]
    m_ref[...] = jnp.full(m_ref.shape, MASK_BIAS, F32)
    l_ref[...] = jnp.zeros(l_ref.shape, F32)
    acc_ref[...] = jnp.zeros(acc_ref.shape, F32)

    def attn_chunk(c, carry):
        keys = keys_ref[c]
        bias = jnp.concatenate(
            [jnp.where(keys[:, j * LANES:(j + 1) * LANES] >= thr_b, 0.0, MASK_BIAS)
             for j in range(sc // LANES)], axis=-1)
        bias = jnp.concatenate([bias] * HEADS_PER_UNIT, axis=0)
        start = pl.multiple_of(c * sc, sc)
        for u, (kv, g0) in enumerate(units):
            qc = jnp.concatenate(
                [q_ref[:, (kv * N_GROUP + g) * HEAD_DIM:(kv * N_GROUP + g + 1) * HEAD_DIM]
                 for g in range(g0, g0 + HEADS_PER_UNIT)], axis=0)
            kc = k_ref[pl.ds(start, sc), kv * HEAD_DIM:(kv + 1) * HEAD_DIM]
            vc = v_ref[pl.ds(start, sc), kv * HEAD_DIM:(kv + 1) * HEAD_DIM]
            _softmax_step(_dot_nt(qc, kc) + bias, m_ref, l_ref, acc_ref, vc, u)
        return carry

    lax.fori_loop(0, n_chunks, attn_chunk, 0)

    for u, (kv, g0) in enumerate(units):
        o = acc_ref[u] / l_ref[u]
        for j in range(HEADS_PER_UNIT):
            hd = kv * N_GROUP + g0 + j
            o_ref[:, hd * HEAD_DIM:(hd + 1) * HEAD_DIM] = o[j * qb:(j + 1) * qb, :].astype(o_ref.dtype)


def _attn_prompt(qi, kw, q, ki2, kb, vb, qb, sc):
    B, S, _ = q.shape
    k_top = min(TOPK_MAX, S // 4)
    assert sc >= k_top and S % sc == 0 and S % qb == 0 and sc % qb == 0
    qblk = lambda n: pl.BlockSpec((None, qb, n), lambda b, i: (b, i, 0))
    whole = lambda n: pl.BlockSpec((None, S, n), lambda b, i: (b, 0, 0), pipeline_mode=pl.Buffered(1))
    rows = HEADS_PER_UNIT * qb
    n_units = N_HEADS // HEADS_PER_UNIT
    return pl.pallas_call(
        functools.partial(_attn_prompt_kernel, qb=qb, sc=sc, k_top=k_top,
                          index_scale=(N_IDX_HEADS * IDX_DIM) ** -0.5),
        grid=(B, S // qb),
        in_specs=[qblk(qi.shape[2]), qblk(kw.shape[2]), qblk(q.shape[2]),
                  whole(ki2.shape[2]), whole(kb.shape[2]), whole(vb.shape[2])],
        out_specs=qblk(q.shape[2]),
        out_shape=jax.ShapeDtypeStruct(q.shape, BF16),
        scratch_shapes=[
            pltpu.VMEM((S // sc, qb, sc), I32),
            pltpu.VMEM((N_IDX_HEADS, qb, LANES), BF16),
            pltpu.VMEM((N_IDX_HEADS, qb, LANES), F32),
            pltpu.VMEM((3, qb, LANES), I32),
            pltpu.VMEM((n_units, rows, LANES), F32),
            pltpu.VMEM((n_units, rows, LANES), F32),
            pltpu.VMEM((n_units, rows, HEAD_DIM), F32),
        ],
        compiler_params=_params("parallel", "arbitrary"),
        name="attn_prompt",
    )(qi, kw, q, ki2, kb, vb)


def _attn_sample_kernel(pt_ref, qi_ref, kw_ref, q_ref, kin_ref, kn_ref, vn_ref,
                        ci_hbm, ck_hbm, cv_hbm, o_ref,
                        keys_ref, keyn_ref, ibuf, kbuf, vbuf, m_ref, l_ref, acc_ref,
                        ipg, kpg, vpg, sem_i, sem_k, sem_v,
                        *, n_pg, n_steps, t_new, page, k_top, index_scale):
    b = pl.program_id(0)
    sc = n_pg * page

    def group_copies(src, dst, sem, step, slot):
        return [pltpu.make_async_copy(src.at[pt_ref[b, step * n_pg + p]], dst.at[slot, p], sem.at[slot])
                for p in range(n_pg)]

    def start_group(src, dst, sem, step, slot):
        for cp in group_copies(src, dst, sem, step, slot):
            cp.start()

    def wait_group(src, dst, sem, step, slot):
        for cp in group_copies(src, dst, sem, step, slot):
            cp.wait()

    idx_slots = ipg.shape[0]
    kv_slots = kpg.shape[0]
    for s0 in range(min(idx_slots, n_steps)):
        start_group(ci_hbm, ipg, sem_i, s0, s0)
    for s0 in range(min(kv_slots, n_steps)):
        start_group(ck_hbm, kpg, sem_k, s0, s0)
        start_group(cv_hbm, vpg, sem_v, s0, s0)

    def head_sum(x):
        return jnp.sum(x.reshape(N_IDX_HEADS, t_new, x.shape[-1]), axis=0)

    qi = qi_ref[...].astype(F32)
    lane = lax.broadcasted_iota(I32, (t_new, LANES), 1)
    pieces, wcols = [], []
    for h in range(N_IDX_HEADS):
        blk = qi[:, (h // 2) * LANES:(h // 2 + 1) * LANES]
        keep = (lane < IDX_DIM) if h % 2 == 0 else (lane >= IDX_DIM)
        pieces.append(jnp.where(keep, blk, 0.0))
        w_h = kw_ref[:, IDX_DIM + h:IDX_DIM + h + 1] * index_scale
        wcols.append(jnp.broadcast_to(w_h, (t_new, LANES)))
    lhs = jnp.concatenate(pieces, axis=0).astype(BF16)
    w_b = jnp.concatenate(wcols, axis=0)

    def scores(key_mat):
        logit = _dot(lhs, key_mat)
        parts = [head_sum(jnp.maximum(logit[:, c * LANES:(c + 1) * LANES], 0.0) * w_b)
                 for c in range(key_mat.shape[1] // LANES)]
        return jnp.concatenate(parts, axis=-1)

    def score_step(s, carry):
        slot = s % idx_slots
        wait_group(ci_hbm, ipg, sem_i, s, slot)
        for p in range(n_pg):
            pg = ipg[slot, p].astype(BF16)
            ibuf[:, p * page:(p + 1) * page] = jnp.concatenate([pg, pg], axis=0)

        @pl.when(s + idx_slots < n_steps)
        def _():
            start_group(ci_hbm, ipg, sem_i, s + idx_slots, slot)

        keys_ref[s] = _sort_key(scores(ibuf[...]))
        return carry

    lax.fori_loop(0, n_steps, score_step, 0)

    s_new = scores(kin_ref[...])
    t_pos = lax.broadcasted_iota(I32, (t_new, LANES), 0)
    s_pos = lax.broadcasted_iota(I32, (t_new, LANES), 1)
    keyn_ref[...] = jnp.where(s_pos <= t_pos, _sort_key(s_new), NEG_INF_KEY)

    past = n_steps * sc

    def fold(f, init):
        carry = init
        for c in range(n_steps):
            carry = f(carry, keys_ref[c], c * sc)
        return f(carry, keyn_ref[...], past)

    def update(f):
        for c in range(n_steps):
            keys_ref[c] = f(keys_ref[c], c * sc)
        keyn_ref[...] = f(keyn_ref[...], past)

    thr = _topk_threshold(fold, update, t_new, k_top, (past + LANES).bit_length())
    thr_b = jnp.broadcast_to(thr, (t_new, LANES))
    m_ref[...] = jnp.full(m_ref.shape, MASK_BIAS, F32)
    l_ref[...] = jnp.zeros(l_ref.shape, F32)
    acc_ref[...] = jnp.zeros(acc_ref.shape, F32)

    def attend(keys, k_of, v_of):
        bias = jnp.concatenate(
            [jnp.where(keys[:, c * LANES:(c + 1) * LANES] >= thr_b, 0.0, MASK_BIAS)
             for c in range(keys.shape[1] // LANES)], axis=-1)
        bias = jnp.concatenate([bias] * N_GROUP, axis=0)
        q = q_ref[...].astype(F32)
        for kv in range(N_KV_HEADS):
            qc = jnp.concatenate(
                [q[:, (kv * N_GROUP + g) * HEAD_DIM:(kv * N_GROUP + g + 1) * HEAD_DIM]
                 for g in range(N_GROUP)], axis=0).astype(BF16)
            _softmax_step(_dot_nt(qc, k_of(kv)) + bias, m_ref, l_ref, acc_ref, v_of(kv), kv)

    def attend_step(s, carry):
        slot = s % kv_slots
        wait_group(ck_hbm, kpg, sem_k, s, slot)
        wait_group(cv_hbm, vpg, sem_v, s, slot)
        for p in range(n_pg):
            for kv in range(N_KV_HEADS):
                rows_kv = pl.ds(kv, page, stride=N_KV_HEADS)
                kbuf[kv, p * page:(p + 1) * page, :] = kpg[slot, p, rows_kv, :].astype(BF16)
                vbuf[kv, p * page:(p + 1) * page, :] = vpg[slot, p, rows_kv, :].astype(BF16)

        @pl.when(s + kv_slots < n_steps)
        def _():
            start_group(ck_hbm, kpg, sem_k, s + kv_slots, slot)
            start_group(cv_hbm, vpg, sem_v, s + kv_slots, slot)

        attend(keys_ref[s], lambda kv: kbuf[kv], lambda kv: vbuf[kv])
        return carry

    lax.fori_loop(0, n_steps, attend_step, 0)

    attend(keyn_ref[...],
           lambda kv: kn_ref[:, kv * HEAD_DIM:(kv + 1) * HEAD_DIM],
           lambda kv: vn_ref[:, kv * HEAD_DIM:(kv + 1) * HEAD_DIM])
    for kv in range(N_KV_HEADS):
        o = acc_ref[kv] / l_ref[kv]
        for g in range(N_GROUP):
            hd = kv * N_GROUP + g
            o_ref[:, hd * HEAD_DIM:(hd + 1) * HEAD_DIM] = (
                o[g * t_new:(g + 1) * t_new, :].astype(o_ref.dtype))


def _attn_sample(page_table, qi, kw, q, ki2n_t, kbn, vbn, cache_ik_t, cache_k, cache_v, n_pg):
    B, T, _ = q.shape
    n_pages = page_table.shape[1]
    page = cache_ik_t.shape[2]
    assert n_pages % n_pg == 0 and page == LANES and cache_k.shape[1] == page * N_KV_HEADS
    assert cache_ik_t.shape[1] == IDX_DIM
    n_steps = n_pages // n_pg
    kv_slots = min(KV_SLOTS, n_steps)
    sc = n_pg * page
    k_top = min(TOPK_MAX, (n_pages * page + T) // 4)
    rows = N_GROUP * T

    per_b = lambda a: pl.BlockSpec((None,) + a.shape[1:], lambda b, pt: (b, 0, 0))
    hbm = pl.BlockSpec(memory_space=pl.ANY)
    grid_spec = pltpu.PrefetchScalarGridSpec(
        num_scalar_prefetch=1,
        grid=(B,),
        in_specs=[per_b(qi), per_b(kw), per_b(q), per_b(ki2n_t), per_b(kbn), per_b(vbn), hbm, hbm, hbm],
        out_specs=pl.BlockSpec((None, T, q.shape[2]), lambda b, pt: (b, 0, 0)),
        scratch_shapes=[
            pltpu.VMEM((n_steps, T, sc), I32),
            pltpu.VMEM((T, LANES), I32),
            pltpu.VMEM((2 * IDX_DIM, sc), BF16),
            pltpu.VMEM((N_KV_HEADS, sc, HEAD_DIM), BF16),
            pltpu.VMEM((N_KV_HEADS, sc, HEAD_DIM), BF16),
            pltpu.VMEM((N_KV_HEADS, rows, LANES), F32),
            pltpu.VMEM((N_KV_HEADS, rows, LANES), F32),
            pltpu.VMEM((N_KV_HEADS, rows, HEAD_DIM), F32),
            pltpu.VMEM((kv_slots, n_pg) + cache_ik_t.shape[1:], cache_ik_t.dtype),
            pltpu.VMEM((kv_slots, n_pg) + cache_k.shape[1:], cache_k.dtype),
            pltpu.VMEM((kv_slots, n_pg) + cache_v.shape[1:], cache_v.dtype),
            pltpu.SemaphoreType.DMA((kv_slots,)),
            pltpu.SemaphoreType.DMA((kv_slots,)),
            pltpu.SemaphoreType.DMA((kv_slots,)),
        ],
    )
    return pl.pallas_call(
        functools.partial(_attn_sample_kernel, n_pg=n_pg, n_steps=n_steps, t_new=T, page=page,
                          k_top=k_top, index_scale=(N_IDX_HEADS * IDX_DIM) ** -0.5),
        grid_spec=grid_spec,
        out_shape=jax.ShapeDtypeStruct(q.shape, BF16),
        compiler_params=_params("arbitrary"),
        name="attn_sample",
    )(page_table, qi, kw, q, ki2n_t, kbn, vbn, cache_ik_t, cache_k, cache_v)


def _mix_kernel(x_ref, c_ref, a_ref, w_ref, g1_ref, g2_ref, x1_ref, h_ref, *, c_conv):
    tm = x_ref.shape[0]
    for r0 in range(0, tm, tm // 2):
        rs = slice(r0, r0 + tm // 2)
        mixed = _dot(c_ref[rs, :], w_ref[0:c_conv, :]) + _dot(a_ref[rs, :], w_ref[c_conv:, :])
        x1 = x_ref[rs, :] + _rms(mixed, g1_ref[...])
        x1_ref[rs, :] = x1
        h_ref[rs, :] = _rms(x1, g2_ref[...]).astype(BF16)


def _mix(x, conv_o, attn_o, w_out, g_post, g_pre, tm):
    T, D = x.shape
    row = lambda n: pl.BlockSpec((tm, n), lambda i: (i, 0))
    return pl.pallas_call(
        functools.partial(_mix_kernel, c_conv=conv_o.shape[1]),
        grid=(T // tm,),
        in_specs=[row(D), row(conv_o.shape[1]), row(attn_o.shape[1]),
                  _resident(w_out.shape), _resident((1, D)), _resident((1, D))],
        out_specs=(row(D), row(D)),
        out_shape=(jax.ShapeDtypeStruct((T, D), F32), jax.ShapeDtypeStruct((T, D), BF16)),
        compiler_params=_params("parallel"),
        name="mix",
    )(x, conv_o, attn_o, w_out, g_post, g_pre)


def _ffn_kernel(x_ref, h_ref, wg_ref, wu_ref, wd_ref, g_ref, y_ref, acc_ref):
    j = pl.program_id(1)

    @pl.when(j == 0)
    def _():
        acc_ref[...] = jnp.zeros(acc_ref.shape, F32)

    h = h_ref[...]
    gate = _dot(h, wg_ref[...])
    a = (gate * _sigmoid(gate) * _dot(h, wu_ref[...])).astype(BF16)
    acc_ref[...] += _dot(a, wd_ref[...])

    @pl.when(j == pl.num_programs(1) - 1)
    def _():
        y_ref[...] = x_ref[...] + _rms(acc_ref[...], g_ref[...])


def _ffn(x1, h, w_gate, w_up, w_down, g_post, tm, tf):
    T, D = x1.shape
    FF = w_gate.shape[1]
    assert FF % tf == 0
    row = lambda: pl.BlockSpec((tm, D), lambda i, j: (i, 0))
    return pl.pallas_call(
        _ffn_kernel,
        grid=(T // tm, FF // tf),
        in_specs=[row(), row(),
                  pl.BlockSpec((D, tf), lambda i, j: (0, j)),
                  pl.BlockSpec((D, tf), lambda i, j: (0, j)),
                  pl.BlockSpec((tf, D), lambda i, j: (j, 0)),
                  _resident((1, D))],
        out_specs=row(),
        out_shape=jax.ShapeDtypeStruct((T, D), F32),
        scratch_shapes=[pltpu.VMEM((tm, D), F32)],
        compiler_params=_params("parallel", "arbitrary"),
        name="ffn",
    )(x1, h, w_gate, w_up, w_down, g_post)


def _tile(n, pref):
    t = min(pref, n)
    while n % t:
        t //= 2
    return t


def _ffn_tile(ff):
    for t in (512, 256, 128):
        if ff % t == 0:
            return t
    return ff


def _dup_lanes(ki):
    return jnp.concatenate([ki, ki], axis=-1).astype(BF16)


def kernel(x_prompt, x_sample, cache_k, cache_v, cache_idx_k, state_conv, page_table,
           g_pre_mix, w_in, conv_w, conv_b, conv_ln_g, conv_ln_b, w_out,
           g_post_mix, g_pre_ffn, w_gate, w_up, w_down, g_post_ffn):
    B, S, D = x_prompt.shape
    Bd, Td, _ = x_sample.shape
    depth = w_in.shape[0]
    c_conv = conv_w.shape[2]
    c_q = N_HEADS * HEAD_DIM
    c_kv = N_KV_HEADS * HEAD_DIM
    c_qi = N_IDX_HEADS * IDX_DIM
    sizes = (c_conv, c_conv, c_q, c_kv, c_kv, c_qi, IDX_DIM, N_IDX_HEADS)
    n_in = sum(sizes)
    assert w_in.shape[2] == n_in
    n_pad = -n_in % LANES
    n_pool, page = cache_k.shape[1], cache_k.shape[2]

    xp = x_prompt.reshape(B * S, D)
    xs = x_sample.reshape(Bd * Td, D)
    outs = [[] for _ in range(8)]
    row2 = lambda a: a.reshape(1, -1)

    for l in range(depth):
        w_in_b = jnp.pad(w_in[l], ((0, 0), (0, n_pad))).astype(BF16)
        w_out_b = w_out[l].astype(BF16)
        wg_b, wu_b, wd_b = w_gate[l].astype(BF16), w_up[l].astype(BF16), w_down[l].astype(BF16)
        g_pre, g_post, g_ffn_pre, g_ffn_post = (row2(g_pre_mix[l]), row2(g_post_mix[l]),
                                                row2(g_pre_ffn[l]), row2(g_post_ffn[l]))
        cw, cb, lg, lb = conv_w[l], row2(conv_b[l]), row2(conv_ln_g[l]), row2(conv_ln_b[l])
        tf = _ffn_tile(wg_b.shape[1])

        tm = _tile(B * S, 512)
        u, q, k, v, kb, vb, qi, kw = _proj(xp, g_pre, w_in_b, sizes, tm)
        r3 = lambda a: a.reshape(B, S, a.shape[-1])
        conv_o = _conv_prompt(r3(u), cw, cb, lg, lb, _tile(S, 256))
        ki = kw[:, :IDX_DIM]
        attn_o = _attn_prompt(r3(qi), r3(kw), r3(q), r3(_dup_lanes(ki)), r3(kb), r3(vb),
                              qb=_tile(S, 512), sc=_tile(S, 512))
        x1, h2 = _mix(xp, conv_o.reshape(B * S, c_conv), attn_o.reshape(B * S, c_q),
                      w_out_b, g_post, g_ffn_pre, tm)
        xp = _ffn(x1, h2, wg_b, wu_b, wd_b, g_ffn_post, tm, tf)
        outs[0].append(k.reshape(B, S, N_KV_HEADS, HEAD_DIM))
        outs[1].append(v.reshape(B, S, N_KV_HEADS, HEAD_DIM))
        outs[2].append(ki.reshape(B, S, IDX_DIM))
        outs[3].append(r3(u)[:, S - (cw.shape[0] - 1):, :])

        tms = _tile(Bd * Td, 256)
        u, q, k, v, kb, vb, qi, kw = _proj(xs, g_pre, w_in_b, sizes, tms)
        r3 = lambda a: a.reshape(Bd, Td, a.shape[-1])
        conv_o, conv_st = _conv_sample(r3(u), state_conv[l], cw, cb, lg, lb)
        ki = kw[:, :IDX_DIM]
        pad_rows = lambda a: jnp.pad(r3(a), ((0, 0), (0, LANES - Td), (0, 0)))
        ki_t = jnp.pad(jnp.swapaxes(r3(ki), 1, 2), ((0, 0), (0, 0), (0, LANES - Td))).astype(BF16)
        attn_o = _attn_sample(
            page_table, r3(qi), r3(kw), r3(q), jnp.concatenate([ki_t, ki_t], axis=1),
            pad_rows(kb), pad_rows(vb),
            jnp.swapaxes(cache_idx_k[l], 1, 2),

            cache_k[l].reshape(n_pool, page * N_KV_HEADS, HEAD_DIM),
            cache_v[l].reshape(n_pool, page * N_KV_HEADS, HEAD_DIM),
            n_pg=_tile(page_table.shape[1], 16))
        x1, h2 = _mix(xs, conv_o.reshape(Bd * Td, c_conv), attn_o.reshape(Bd * Td, c_q),
                      w_out_b, g_post, g_ffn_pre, tms)
        xs = _ffn(x1, h2, wg_b, wu_b, wd_b, g_ffn_post, tms, tf)
        outs[4].append(k.reshape(Bd, Td, N_KV_HEADS, HEAD_DIM))
        outs[5].append(v.reshape(Bd, Td, N_KV_HEADS, HEAD_DIM))
        outs[6].append(ki.reshape(Bd, Td, IDX_DIM))
        outs[7].append(conv_st)

    return (xp.reshape(B, S, D), xs.reshape(Bd, Td, D), *[jnp.stack(o) for o in outs])
```

```python
import functools

import jax
import jax.numpy as jnp
from jax import lax
from jax.experimental import pallas as pl
from jax.experimental.pallas import tpu as pltpu

N_HEADS = 8
N_KV_HEADS = 2
N_GROUP = N_HEADS // N_KV_HEADS
HEAD_DIM = 128
N_IDX_HEADS = 16
IDX_DIM = 64
TOPK_MAX = 256
EPS = 1e-6
LANES = 128
SUBLANES = 8
VMEM_LIMIT = 56 * 1024 * 1024

INT_MIN = -2 ** 31
NEG_INF_KEY = (0xFF800000 ^ 0x7FFFFFFF) - 2 ** 32
MASK_BIAS = -1e30
COUNT_ROWS = 64
KV_SLOTS = 4
IDX_SLOTS = 8
HEADS_PER_UNIT = 4
LOG2_E = 1.4426950408889634

F32 = jnp.float32
BF16 = jnp.bfloat16
I32 = jnp.int32


def _dot(a, b):
    return jnp.dot(a, b, preferred_element_type=F32)


def _dot_nt(a, b):
    return lax.dot_general(a, b, (((1,), (1,)), ((), ())), preferred_element_type=F32)


def _sigmoid(x):
    return 1.0 / (1.0 + jnp.exp(-x))


def _rms(x, g):
    return x * lax.rsqrt(jnp.mean(x * x, axis=-1, keepdims=True) + EPS) * g


def _sort_key(x):
    b = pltpu.bitcast(x, I32)
    return b ^ ((b >> 31) & 0x7FFFFFFF)


def _params(*sem):
    return pltpu.CompilerParams(dimension_semantics=sem, vmem_limit_bytes=VMEM_LIMIT)


def _resident(shape):
    nd = len(shape)
    return pl.BlockSpec(shape, lambda *_: (0,) * nd, pipeline_mode=pl.Buffered(1))


def _proj_kernel(x_ref, g_ref, w_ref, u_ref, q_ref, k_ref, v_ref, kb_ref, vb_ref,
                 qi_ref, kw_ref, *, offs, attn_scale):
    h = _rms(x_ref[...], g_ref[...]).astype(BF16)
    o_ua, o_ug, o_q, o_k, o_v, o_qi, o_kw, o_end = offs
    ua = _dot(h, w_ref[:, o_ua:o_ug])
    ug = _dot(h, w_ref[:, o_ug:o_q])
    u_ref[...] = ua * _sigmoid(ug)
    q_ref[...] = (_dot(h, w_ref[:, o_q:o_k]) * attn_scale).astype(BF16)
    k = _dot(h, w_ref[:, o_k:o_v])
    k_ref[...] = k
    kb_ref[...] = k.astype(BF16)
    v = _dot(h, w_ref[:, o_v:o_qi])
    v_ref[...] = v
    vb_ref[...] = v.astype(BF16)
    qi_ref[...] = _dot(h, w_ref[:, o_qi:o_kw]).astype(BF16)
    kw_ref[...] = _dot(h, w_ref[:, o_kw:o_end])


def _proj(x, g, w, sizes, tm):
    T, D = x.shape
    c_conv, _, c_q, c_kv, _, c_qi, c_ki, c_wi = sizes
    offs = [0]
    for s in sizes[:6]:
        offs.append(offs[-1] + s)
    offs.append(w.shape[1])
    kw_w = offs[-1] - offs[-2]
    row = lambda n: pl.BlockSpec((tm, n), lambda i: (i, 0))
    out_shape = (
        jax.ShapeDtypeStruct((T, c_conv), F32),
        jax.ShapeDtypeStruct((T, c_q), BF16),
        jax.ShapeDtypeStruct((T, c_kv), F32),
        jax.ShapeDtypeStruct((T, c_kv), F32),
        jax.ShapeDtypeStruct((T, c_kv), BF16),
        jax.ShapeDtypeStruct((T, c_kv), BF16),
        jax.ShapeDtypeStruct((T, c_qi), BF16),
        jax.ShapeDtypeStruct((T, kw_w), F32),
    )
    return pl.pallas_call(
        functools.partial(_proj_kernel, offs=tuple(offs), attn_scale=HEAD_DIM ** -0.5 * LOG2_E),
        grid=(T // tm,),
        in_specs=[row(D), _resident((1, D)), _resident(w.shape)],
        out_specs=tuple(row(s.shape[1]) for s in out_shape),
        out_shape=out_shape,
        compiler_params=_params("parallel"),
        name="proj",
    )(x, g, w)


HIST_PAD = 32


def _conv_body(full_ref, sh_ref, y_ref, w_ref, b_ref, g_ref, beta_ref, o_ref, *, width, tb, cw):
    C = full_ref.shape[1]
    base = HIST_PAD - (width - 1)
    n_b = min(SUBLANES, width)
    n_a = [(width - 1 - b) // SUBLANES + 1 for b in range(n_b)]
    for b in range(n_b):
        rows_b = tb + SUBLANES * (n_a[b] - 1)
        sh_ref[b, 0:rows_b, :] = full_ref[base + b:base + b + rows_b, :]
    for c in range(C // cw):
        cs = slice(c * cw, (c + 1) * cw)
        acc = jnp.zeros((tb, cw), F32)
        for b in range(n_b):
            for a in range(n_a[b]):
                j = b + SUBLANES * a
                acc = acc + sh_ref[b, SUBLANES * a:SUBLANES * a + tb, cs] * w_ref[j:j + 1, cs]
        y_ref[:, cs] = acc + b_ref[:, cs]
    y = y_ref[...]
    mu = jnp.mean(y, axis=-1, keepdims=True)
    yc = y - mu
    z = yc * lax.rsqrt(jnp.mean(yc * yc, axis=-1, keepdims=True) + EPS) * g_ref[...] + beta_ref[...]
    o_ref[...] = (z * _sigmoid(z)).astype(o_ref.dtype)


def _conv_scratch(tb, C, width):
    n_b = min(SUBLANES, width)
    return [pltpu.VMEM((HIST_PAD + tb, C), F32),
            pltpu.VMEM((n_b, tb + SUBLANES * ((width - 1) // SUBLANES), C), F32),
            pltpu.VMEM((tb, C), F32)]


def _conv_prompt_kernel(prev_ref, u_ref, w_ref, b_ref, g_ref, beta_ref, o_ref,
                        full_ref, sh_ref, y_ref, *, width, tb, cw):
    @pl.when(pl.program_id(1) == 0)
    def _():
        full_ref[0:HIST_PAD, :] = jnp.zeros((HIST_PAD, full_ref.shape[1]), F32)

    @pl.when(pl.program_id(1) > 0)
    def _():
        full_ref[0:HIST_PAD, :] = prev_ref[...]

    full_ref[HIST_PAD:HIST_PAD + tb, :] = u_ref[...]
    _conv_body(full_ref, sh_ref, y_ref, w_ref, b_ref, g_ref, beta_ref, o_ref, width=width, tb=tb, cw=cw)


def _conv_prompt(u, conv_w, conv_b, ln_g, ln_b, tb):
    B, S, C = u.shape
    width = conv_w.shape[0]
    r = tb // HIST_PAD
    vec = lambda: _resident((1, C))
    return pl.pallas_call(
        functools.partial(_conv_prompt_kernel, width=width, tb=tb, cw=128),
        grid=(B, S // tb),
        in_specs=[
            pl.BlockSpec((None, HIST_PAD, C), lambda b, i: (b, jnp.maximum(i * r - 1, 0), 0)),
            pl.BlockSpec((None, tb, C), lambda b, i: (b, i, 0)),
            _resident(conv_w.shape), vec(), vec(), vec(),
        ],
        out_specs=pl.BlockSpec((None, tb, C), lambda b, i: (b, i, 0)),
        out_shape=jax.ShapeDtypeStruct((B, S, C), BF16),
        scratch_shapes=_conv_scratch(tb, C, width),
        compiler_params=_params("parallel", "arbitrary"),
        name="conv_prompt",
    )(u, u, conv_w, conv_b, ln_g, ln_b)


def _conv_sample_kernel(hist_ref, u_ref, w_ref, b_ref, g_ref, beta_ref, o_ref, st_ref,
                        full_ref, sh_ref, y_ref, *, width, tb, cw):
    full_ref[0:HIST_PAD, :] = hist_ref[...]
    full_ref[HIST_PAD:HIST_PAD + tb, :] = u_ref[...]
    _conv_body(full_ref, sh_ref, y_ref, w_ref, b_ref, g_ref, beta_ref, o_ref, width=width, tb=tb, cw=cw)
    st_ref[...] = full_ref[HIST_PAD + tb - (width - 1):HIST_PAD + tb, :]


def _conv_sample(u, hist, conv_w, conv_b, ln_g, ln_b):
    B, T, C = u.shape
    width = conv_w.shape[0]
    hist_p = jnp.pad(hist, ((0, 0), (HIST_PAD - (width - 1), 0), (0, 0)))
    vec = lambda: _resident((1, C))
    return pl.pallas_call(
        functools.partial(_conv_sample_kernel, width=width, tb=T, cw=256),
        grid=(B,),
        in_specs=[
            pl.BlockSpec((None, HIST_PAD, C), lambda b: (b, 0, 0)),
            pl.BlockSpec((None, T, C), lambda b: (b, 0, 0)),
            _resident(conv_w.shape), vec(), vec(), vec(),
        ],
        out_specs=(pl.BlockSpec((None, T, C), lambda b: (b, 0, 0)),
                   pl.BlockSpec((None, width - 1, C), lambda b: (b, 0, 0))),
        out_shape=(jax.ShapeDtypeStruct((B, T, C), BF16),
                   jax.ShapeDtypeStruct((B, width - 1, C), F32)),
        scratch_shapes=_conv_scratch(T, C, width),
        compiler_params=_params("parallel"),
        name="conv_sample",
    )(hist_p, u, conv_w, conv_b, ln_g, ln_b)


def _count(fold, rows, pred, row_vals, stage_ref):
    grouped = rows > COUNT_ROWS
    if grouped:
        for n, v in enumerate(row_vals):
            stage_ref[n] = jnp.broadcast_to(v, (rows, LANES))

    def group(cnt, keys, pos0, vals):
        hits = [jnp.where(pred(keys[:, j * LANES:(j + 1) * LANES], pos0 + j * LANES, *vals), 1, 0)
                for j in range(keys.shape[1] // LANES)]
        while len(hits) > 1:
            hits = [a + b for a, b in zip(hits[::2], hits[1::2])] + hits[len(hits) & ~1:]
        return cnt + hits[0]

    def add_block(cnt, keys, pos0):
        if not grouped:
            return group(cnt, keys, pos0, [jnp.broadcast_to(v, (rows, LANES)) for v in row_vals])
        return jnp.concatenate(
            [group(cnt[r:r + COUNT_ROWS], keys[r:r + COUNT_ROWS], pos0,
                   [stage_ref[n, r:r + COUNT_ROWS, :] for n in range(len(row_vals))])
             for r in range(0, rows, COUNT_ROWS)], axis=0)

    cnt = fold(add_block, jnp.zeros((rows, LANES), I32))
    return jnp.sum(cnt.astype(F32), axis=-1, keepdims=True).astype(I32)


def _topk_threshold(fold, update, rows, k_top, n_pos_bits, stage_ref=None):
    def bcast(x):
        return jnp.broadcast_to(x, (rows, LANES))

    def count_ge(cand):
        return _count(fold, rows, lambda keys, pos0, c: keys >= c, [cand], stage_ref)

    def bit_step(it, state):
        ub, c_ub = state
        cand_u = ub | jnp.left_shift(jnp.int32(1), 31 - it)
        cnt = count_ge(cand_u ^ INT_MIN)
        keep = cnt >= k_top
        return jnp.where(keep, cand_u, ub), jnp.where(keep, cnt, c_ub)

    start = (jnp.zeros((rows, 1), I32), jnp.full((rows, 1), k_top + 1, I32))
    ub, c_thr = lax.fori_loop(0, 32, bit_step, start)
    thr = ub ^ INT_MIN
    tie = (c_thr > k_top) & (thr > NEG_INF_KEY)

    @pl.when(jnp.sum(jnp.where(tie, 1.0, 0.0)) > 0.0)
    def _():
        need = k_top - count_ge(thr + 1)
        thr_b = bcast(thr)
        lane = lax.broadcasted_iota(I32, (rows, LANES), 1)

        def tied_before(cut):
            return _count(fold, rows, lambda keys, pos0, t, c, ln: (keys == t) & (ln + pos0 < c),
                          [thr, cut, lane], stage_ref)

        def cut_step(it, cut):
            cand = cut | jnp.left_shift(jnp.int32(1), n_pos_bits - 1 - it)
            return jnp.where(tied_before(cand) <= need, cand, cut)

        cut = lax.fori_loop(0, n_pos_bits, cut_step, jnp.zeros((rows, 1), I32))
        cut_b = bcast(jnp.where(tie, cut, 2 ** n_pos_bits))

        def drop(keys, pos0):
            parts = []
            for j in range(keys.shape[1] // LANES):
                blk = keys[:, j * LANES:(j + 1) * LANES]
                surplus = (blk == thr_b) & (lane + (pos0 + j * LANES) >= cut_b)
                parts.append(jnp.where(surplus, NEG_INF_KEY, blk))
            return jnp.concatenate(parts, axis=-1)

        update(drop)

    return jnp.maximum(thr, NEG_INF_KEY + 1)


def _softmax_step(s, m_ref, l_ref, acc_ref, v, c):
    m_old = m_ref[c]
    m_new = jnp.maximum(m_old, jnp.max(s, axis=-1, keepdims=True))
    alpha = jnp.exp2(m_old - m_new)
    p = jnp.exp2(s - jnp.tile(m_new, (1, s.shape[1] // LANES)))
    l_ref[c] = alpha * l_ref[c] + jnp.sum(p, axis=-1, keepdims=True)
    acc_ref[c] = alpha * acc_ref[c] + _dot(p.astype(BF16), v)
    m_ref[c] = m_new


def _attn_prompt_kernel(qi_ref, kw_ref, q_ref, ki_ref, k_ref, v_ref, o_ref,
                        keys_ref, qh_ref, wb_ref, stage_ref, m_ref, l_ref, acc_ref,
                        *, qb, sc, k_top, index_scale):
    i = pl.program_id(1)
    n_pos_bits = (keys_ref.shape[0] * sc).bit_length()
    n_chunks = ((i + 1) * qb + sc - 1) // sc

    lane = lax.broadcasted_iota(I32, (qb, LANES), 1)
    for h in range(N_IDX_HEADS):
        blk = qi_ref[:, (h // 2) * LANES:(h // 2 + 1) * LANES]
        keep = (lane < IDX_DIM) if h % 2 == 0 else (lane >= IDX_DIM)
        qh_ref[h] = jnp.where(keep, blk, jnp.zeros_like(blk))
        w_h = kw_ref[:, IDX_DIM + h:IDX_DIM + h + 1] * index_scale
        wb_ref[h] = jnp.broadcast_to(w_h, (qb, LANES))

    q_pos = i * qb + lax.broadcasted_iota(I32, (qb, sc), 0)

    def score_chunk(c, carry):
        ks = ki_ref[pl.ds(pl.multiple_of(c * sc, sc), sc), :]
        parts = [jnp.zeros((qb, LANES), F32) for _ in range(sc // LANES)]
        for h in range(N_IDX_HEADS):
            logit = _dot_nt(qh_ref[h], ks)
            w_b = wb_ref[h]
            for j in range(sc // LANES):
                parts[j] = parts[j] + jnp.maximum(logit[:, j * LANES:(j + 1) * LANES], 0.0) * w_b
        score = jnp.concatenate(parts, axis=-1)
        s_pos = c * sc + lax.broadcasted_iota(I32, (qb, sc), 1)
        keys_ref[c] = jnp.where(s_pos <= q_pos, _sort_key(score), NEG_INF_KEY)
        return carry

    lax.fori_loop(0, n_chunks, score_chunk, 0)

    def fold(f, init):
        return lax.fori_loop(0, n_chunks, lambda c, carry: f(carry, keys_ref[c], c * sc), init)

    def update(f):
        def body(c, carry):
            keys_ref[c] = f(keys_ref[c], c * sc)
            return carry
        lax.fori_loop(0, n_chunks, body, 0)

    thr = _topk_threshold(fold, update, qb, k_top, n_pos_bits, stage_ref)
    thr_b = jnp.broadcast_to(thr, (qb, LANES))

    units = [(kv, g0) for kv in range(N_KV_HEADS) for g0 in range(0, N_GROUP, HEADS_PER_UNIT)]
    m_ref[...] = jnp.full(m_ref.shape, MASK_BIAS, F32)
    l_ref[...] = jnp.zeros(l_ref.shape, F32)
    acc_ref[...] = jnp.zeros(acc_ref.shape, F32)

    def attn_chunk(c, carry):
        keys = keys_ref[c]
        bias = jnp.concatenate(
            [jnp.where(keys[:, j * LANES:(j + 1) * LANES] >= thr_b, 0.0, MASK_BIAS)
             for j in range(sc // LANES)], axis=-1)
        bias = jnp.concatenate([bias] * HEADS_PER_UNIT, axis=0)
        start = pl.multiple_of(c * sc, sc)
        for u, (kv, g0) in enumerate(units):
            qc = jnp.concatenate(
                [q_ref[:, (kv * N_GROUP + g) * HEAD_DIM:(kv * N_GROUP + g + 1) * HEAD_DIM]
                 for g in range(g0, g0 + HEADS_PER_UNIT)], axis=0)
            kc = k_ref[pl.ds(start, sc), kv * HEAD_DIM:(kv + 1) * HEAD_DIM]
            vc = v_ref[pl.ds(start, sc), kv * HEAD_DIM:(kv + 1) * HEAD_DIM]
            _softmax_step(_dot_nt(qc, kc) + bias, m_ref, l_ref, acc_ref, vc, u)
        return carry

    lax.fori_loop(0, n_chunks, attn_chunk, 0)

    for u, (kv, g0) in enumerate(units):
        o = acc_ref[u] / l_ref[u]
        for j in range(HEADS_PER_UNIT):
            hd = kv * N_GROUP + g0 + j
            o_ref[:, hd * HEAD_DIM:(hd + 1) * HEAD_DIM] = o[j * qb:(j + 1) * qb, :].astype(o_ref.dtype)


def _attn_prompt(qi, kw, q, ki2, kb, vb, qb, sc):
    B, S, _ = q.shape
    k_top = min(TOPK_MAX, S // 4)
    assert sc >= k_top and S % sc == 0 and S % qb == 0 and sc % qb == 0
    qblk = lambda n: pl.BlockSpec((None, qb, n), lambda b, i: (b, i, 0))
    whole = lambda n: pl.BlockSpec((None, S, n), lambda b, i: (b, 0, 0), pipeline_mode=pl.Buffered(1))
    rows = HEADS_PER_UNIT * qb
    n_units = N_HEADS // HEADS_PER_UNIT
    return pl.pallas_call(
        functools.partial(_attn_prompt_kernel, qb=qb, sc=sc, k_top=k_top,
                          index_scale=(N_IDX_HEADS * IDX_DIM) ** -0.5),
        grid=(B, S // qb),
        in_specs=[qblk(qi.shape[2]), qblk(kw.shape[2]), qblk(q.shape[2]),
                  whole(ki2.shape[2]), whole(kb.shape[2]), whole(vb.shape[2])],
        out_specs=qblk(q.shape[2]),
        out_shape=jax.ShapeDtypeStruct(q.shape, BF16),
        scratch_shapes=[
            pltpu.VMEM((S // sc, qb, sc), I32),
            pltpu.VMEM((N_IDX_HEADS, qb, LANES), BF16),
            pltpu.VMEM((N_IDX_HEADS, qb, LANES), F32),
            pltpu.VMEM((3, qb, LANES), I32),
            pltpu.VMEM((n_units, rows, LANES), F32),
            pltpu.VMEM((n_units, rows, LANES), F32),
            pltpu.VMEM((n_units, rows, HEAD_DIM), F32),
        ],
        compiler_params=_params("parallel", "arbitrary"),
        name="attn_prompt",
    )(qi, kw, q, ki2, kb, vb)


def _attn_sample_kernel(pt_ref, qi_ref, kw_ref, q_ref, kin_ref, kn_ref, vn_ref,
                        ci_hbm, ck_hbm, cv_hbm, o_ref,
                        keys_ref, keyn_ref, ibuf, kbuf, vbuf, m_ref, l_ref, acc_ref,
                        ipg, kpg, vpg, sem_i, sem_k, sem_v,
                        *, n_pg, n_steps, t_new, page, k_top, index_scale):
    b = pl.program_id(0)
    sc = n_pg * page

    def group_copies(src, dst, sem, step, slot):
        return [pltpu.make_async_copy(src.at[pt_ref[b, step * n_pg + p]], dst.at[slot, p], sem.at[slot])
                for p in range(n_pg)]

    def start_group(src, dst, sem, step, slot):
        for cp in group_copies(src, dst, sem, step, slot):
            cp.start()

    def wait_group(src, dst, sem, step, slot):
        for cp in group_copies(src, dst, sem, step, slot):
            cp.wait()

    idx_slots = ipg.shape[0]
    for s0 in range(min(idx_slots, n_steps)):
        start_group(ci_hbm, ipg, sem_i, s0, s0)
    kv_slots = kpg.shape[0]
    for s0 in range(min(kv_slots - 1, n_steps)):
        start_group(ck_hbm, kpg, sem_k, s0, s0)
        start_group(cv_hbm, vpg, sem_v, s0, s0)

    def head_sum(x):
        return jnp.sum(x.reshape(N_IDX_HEADS, t_new, x.shape[-1]), axis=0)

    qi = qi_ref[...].astype(F32)
    lane = lax.broadcasted_iota(I32, (t_new, LANES), 1)
    pieces, wcols = [], []
    for h in range(N_IDX_HEADS):
        blk = qi[:, (h // 2) * LANES:(h // 2 + 1) * LANES]
        keep = (lane < IDX_DIM) if h % 2 == 0 else (lane >= IDX_DIM)
        pieces.append(jnp.where(keep, blk, 0.0))
        w_h = kw_ref[:, IDX_DIM + h:IDX_DIM + h + 1] * index_scale
        wcols.append(jnp.broadcast_to(w_h, (t_new, LANES)))
    lhs = jnp.concatenate(pieces, axis=0).astype(BF16)
    w_b = jnp.concatenate(wcols, axis=0)

    def scores(key_mat):
        logit = _dot(lhs, key_mat)
        parts = [head_sum(jnp.maximum(logit[:, c * LANES:(c + 1) * LANES], 0.0) * w_b)
                 for c in range(key_mat.shape[1] // LANES)]
        return jnp.concatenate(parts, axis=-1)

    def score_step(s, carry):
        slot = s % idx_slots
        wait_group(ci_hbm, ipg, sem_i, s, slot)
        for p in range(n_pg):
            pg = ipg[slot, p].astype(BF16)
            ibuf[:, p * page:(p + 1) * page] = jnp.concatenate([pg, pg], axis=0)

        @pl.when(s + idx_slots < n_steps)
        def _():
            start_group(ci_hbm, ipg, sem_i, s + idx_slots, slot)

        keys_ref[s] = _sort_key(scores(ibuf[...]))
        return carry

    lax.fori_loop(0, n_steps, score_step, 0)

    s_new = scores(kin_ref[...])
    t_pos = lax.broadcasted_iota(I32, (t_new, LANES), 0)
    s_pos = lax.broadcasted_iota(I32, (t_new, LANES), 1)
    keyn_ref[...] = jnp.where(s_pos <= t_pos, _sort_key(s_new), NEG_INF_KEY)

    past = n_steps * sc

    def fold(f, init):
        carry = init
        for c in range(n_steps):
            carry = f(carry, keys_ref[c], c * sc)
        return f(carry, keyn_ref[...], past)

    def update(f):
        for c in range(n_steps):
            keys_ref[c] = f(keys_ref[c], c * sc)
        keyn_ref[...] = f(keyn_ref[...], past)

    thr = _topk_threshold(fold, update, t_new, k_top, (past + LANES).bit_length())
    thr_b = jnp.broadcast_to(thr, (t_new, LANES))
    m_ref[...] = jnp.full(m_ref.shape, MASK_BIAS, F32)
    l_ref[...] = jnp.zeros(l_ref.shape, F32)
    acc_ref[...] = jnp.zeros(acc_ref.shape, F32)

    def attend(keys, k_of, v_of):
        bias = jnp.concatenate(
            [jnp.where(keys[:, c * LANES:(c + 1) * LANES] >= thr_b, 0.0, MASK_BIAS)
             for c in range(keys.shape[1] // LANES)], axis=-1)
        bias = jnp.concatenate([bias] * N_GROUP, axis=0)
        q = q_ref[...].astype(F32)
        for kv in range(N_KV_HEADS):
            qc = jnp.concatenate(
                [q[:, (kv * N_GROUP + g) * HEAD_DIM:(kv * N_GROUP + g + 1) * HEAD_DIM]
                 for g in range(N_GROUP)], axis=0).astype(BF16)
            _softmax_step(_dot_nt(qc, k_of(kv)) + bias, m_ref, l_ref, acc_ref, v_of(kv), kv)

    def attend_step(s, carry):
        slot = s % kv_slots
        ahead = s + kv_slots - 1

        @pl.when(ahead < n_steps)
        def _():
            start_group(ck_hbm, kpg, sem_k, ahead, ahead % kv_slots)
            start_group(cv_hbm, vpg, sem_v, ahead, ahead % kv_slots)

        wait_group(ck_hbm, kpg, sem_k, s, slot)
        wait_group(cv_hbm, vpg, sem_v, s, slot)
        for p in range(n_pg):
            for kv in range(N_KV_HEADS):
                rows_kv = pl.ds(kv, page, stride=N_KV_HEADS)
                kbuf[kv, p * page:(p + 1) * page, :] = kpg[slot, p, rows_kv, :].astype(BF16)
                vbuf[kv, p * page:(p + 1) * page, :] = vpg[slot, p, rows_kv, :].astype(BF16)
        attend(keys_ref[s], lambda kv: kbuf[kv], lambda kv: vbuf[kv])
        return carry

    lax.fori_loop(0, n_steps, attend_step, 0)

    attend(keyn_ref[...],
           lambda kv: kn_ref[:, kv * HEAD_DIM:(kv + 1) * HEAD_DIM],
           lambda kv: vn_ref[:, kv * HEAD_DIM:(kv + 1) * HEAD_DIM])
    for kv in range(N_KV_HEADS):
        o = acc_ref[kv] / l_ref[kv]
        for g in range(N_GROUP):
            hd = kv * N_GROUP + g
            o_ref[:, hd * HEAD_DIM:(hd + 1) * HEAD_DIM] = (
                o[g * t_new:(g + 1) * t_new, :].astype(o_ref.dtype))


def _attn_sample(page_table, qi, kw, q, ki2n_t, kbn, vbn, cache_ik_t, cache_k, cache_v, n_pg):
    B, T, _ = q.shape
    n_pages = page_table.shape[1]
    page = cache_ik_t.shape[2]
    assert n_pages % n_pg == 0 and page == LANES and cache_k.shape[1] == page * N_KV_HEADS
    assert cache_ik_t.shape[1] == IDX_DIM
    n_steps = n_pages // n_pg
    idx_slots = min(IDX_SLOTS, n_steps)
    sc = n_pg * page
    k_top = min(TOPK_MAX, (n_pages * page + T) // 4)
    rows = N_GROUP * T

    per_b = lambda a: pl.BlockSpec((None,) + a.shape[1:], lambda b, pt: (b, 0, 0))
    hbm = pl.BlockSpec(memory_space=pl.ANY)
    grid_spec = pltpu.PrefetchScalarGridSpec(
        num_scalar_prefetch=1,
        grid=(B,),
        in_specs=[per_b(qi), per_b(kw), per_b(q), per_b(ki2n_t), per_b(kbn), per_b(vbn), hbm, hbm, hbm],
        out_specs=pl.BlockSpec((None, T, q.shape[2]), lambda b, pt: (b, 0, 0)),
        scratch_shapes=[
            pltpu.VMEM((n_steps, T, sc), I32),
            pltpu.VMEM((T, LANES), I32),
            pltpu.VMEM((2 * IDX_DIM, sc), BF16),
            pltpu.VMEM((N_KV_HEADS, sc, HEAD_DIM), BF16),
            pltpu.VMEM((N_KV_HEADS, sc, HEAD_DIM), BF16),
            pltpu.VMEM((N_KV_HEADS, rows, LANES), F32),
            pltpu.VMEM((N_KV_HEADS, rows, LANES), F32),
            pltpu.VMEM((N_KV_HEADS, rows, HEAD_DIM), F32),
            pltpu.VMEM((idx_slots, n_pg) + cache_ik_t.shape[1:], cache_ik_t.dtype),
            pltpu.VMEM((KV_SLOTS, n_pg) + cache_k.shape[1:], cache_k.dtype),
            pltpu.VMEM((KV_SLOTS, n_pg) + cache_v.shape[1:], cache_v.dtype),
            pltpu.SemaphoreType.DMA((idx_slots,)),
            pltpu.SemaphoreType.DMA((KV_SLOTS,)),
            pltpu.SemaphoreType.DMA((KV_SLOTS,)),
        ],
    )
    return pl.pallas_call(
        functools.partial(_attn_sample_kernel, n_pg=n_pg, n_steps=n_steps, t_new=T, page=page,
                          k_top=k_top, index_scale=(N_IDX_HEADS * IDX_DIM) ** -0.5),
        grid_spec=grid_spec,
        out_shape=jax.ShapeDtypeStruct(q.shape, BF16),
        compiler_params=_params("arbitrary"),
        name="attn_sample",
    )(page_table, qi, kw, q, ki2n_t, kbn, vbn, cache_ik_t, cache_k, cache_v)


def _mix_kernel(x_ref, c_ref, a_ref, w_ref, g1_ref, g2_ref, x1_ref, h_ref, *, c_conv):
    tm = x_ref.shape[0]
    for r0 in range(0, tm, tm // 2):
        rs = slice(r0, r0 + tm // 2)
        mixed = _dot(c_ref[rs, :], w_ref[0:c_conv, :]) + _dot(a_ref[rs, :], w_ref[c_conv:, :])
        x1 = x_ref[rs, :] + _rms(mixed, g1_ref[...])
        x1_ref[rs, :] = x1
        h_ref[rs, :] = _rms(x1, g2_ref[...]).astype(BF16)


def _mix(x, conv_o, attn_o, w_out, g_post, g_pre, tm):
    T, D = x.shape
    row = lambda n: pl.BlockSpec((tm, n), lambda i: (i, 0))
    return pl.pallas_call(
        functools.partial(_mix_kernel, c_conv=conv_o.shape[1]),
        grid=(T // tm,),
        in_specs=[row(D), row(conv_o.shape[1]), row(attn_o.shape[1]),
                  _resident(w_out.shape), _resident((1, D)), _resident((1, D))],
        out_specs=(row(D), row(D)),
        out_shape=(jax.ShapeDtypeStruct((T, D), F32), jax.ShapeDtypeStruct((T, D), BF16)),
        compiler_params=_params("parallel"),
        name="mix",
    )(x, conv_o, attn_o, w_out, g_post, g_pre)


def _ffn_kernel(x_ref, h_ref, wg_ref, wu_ref, wd_ref, g_ref, y_ref, acc_ref):
    j = pl.program_id(1)

    @pl.when(j == 0)
    def _():
        acc_ref[...] = jnp.zeros(acc_ref.shape, F32)

    h = h_ref[...]
    gate = _dot(h, wg_ref[...])
    a = (gate * _sigmoid(gate) * _dot(h, wu_ref[...])).astype(BF16)
    acc_ref[...] += _dot(a, wd_ref[...])

    @pl.when(j == pl.num_programs(1) - 1)
    def _():
        y_ref[...] = x_ref[...] + _rms(acc_ref[...], g_ref[...])


def _ffn(x1, h, w_gate, w_up, w_down, g_post, tm, tf):
    T, D = x1.shape
    FF = w_gate.shape[1]
    assert FF % tf == 0
    row = lambda: pl.BlockSpec((tm, D), lambda i, j: (i, 0))
    return pl.pallas_call(
        _ffn_kernel,
        grid=(T // tm, FF // tf),
        in_specs=[row(), row(),
                  pl.BlockSpec((D, tf), lambda i, j: (0, j)),
                  pl.BlockSpec((D, tf), lambda i, j: (0, j)),
                  pl.BlockSpec((tf, D), lambda i, j: (j, 0)),
                  _resident((1, D))],
        out_specs=row(),
        out_shape=jax.ShapeDtypeStruct((T, D), F32),
        scratch_shapes=[pltpu.VMEM((tm, D), F32)],
        compiler_params=_params("parallel", "arbitrary"),
        name="ffn",
    )(x1, h, w_gate, w_up, w_down, g_post)


def _tile(n, pref):
    t = min(pref, n)
    while n % t:
        t //= 2
    return t


def _ffn_tile(ff):
    for t in (512, 256, 128):
        if ff % t == 0:
            return t
    return ff


def _dup_lanes(ki):
    return jnp.concatenate([ki, ki], axis=-1).astype(BF16)


def kernel(x_prompt, x_sample, cache_k, cache_v, cache_idx_k, state_conv, page_table,
           g_pre_mix, w_in, conv_w, conv_b, conv_ln_g, conv_ln_b, w_out,
           g_post_mix, g_pre_ffn, w_gate, w_up, w_down, g_post_ffn):
    B, S, D = x_prompt.shape
    Bd, Td, _ = x_sample.shape
    depth = w_in.shape[0]
    c_conv = conv_w.shape[2]
    c_q = N_HEADS * HEAD_DIM
    c_kv = N_KV_HEADS * HEAD_DIM
    c_qi = N_IDX_HEADS * IDX_DIM
    sizes = (c_conv, c_conv, c_q, c_kv, c_kv, c_qi, IDX_DIM, N_IDX_HEADS)
    n_in = sum(sizes)
    assert w_in.shape[2] == n_in
    n_pad = -n_in % LANES
    n_pool, page = cache_k.shape[1], cache_k.shape[2]

    xp = x_prompt.reshape(B * S, D)
    xs = x_sample.reshape(Bd * Td, D)
    outs = [[] for _ in range(8)]
    row2 = lambda a: a.reshape(1, -1)

    for l in range(depth):
        w_in_b = jnp.pad(w_in[l], ((0, 0), (0, n_pad))).astype(BF16)
        w_out_b = w_out[l].astype(BF16)
        wg_b, wu_b, wd_b = w_gate[l].astype(BF16), w_up[l].astype(BF16), w_down[l].astype(BF16)
        g_pre, g_post, g_ffn_pre, g_ffn_post = (row2(g_pre_mix[l]), row2(g_post_mix[l]),
                                                row2(g_pre_ffn[l]), row2(g_post_ffn[l]))
        cw, cb, lg, lb = conv_w[l], row2(conv_b[l]), row2(conv_ln_g[l]), row2(conv_ln_b[l])
        tf = _ffn_tile(wg_b.shape[1])

        tm = _tile(B * S, 512)
        u, q, k, v, kb, vb, qi, kw = _proj(xp, g_pre, w_in_b, sizes, tm)
        r3 = lambda a: a.reshape(B, S, a.shape[-1])
        conv_o = _conv_prompt(r3(u), cw, cb, lg, lb, _tile(S, 256))
        ki = kw[:, :IDX_DIM]
        attn_o = _attn_prompt(r3(qi), r3(kw), r3(q), r3(_dup_lanes(ki)), r3(kb), r3(vb),
                              qb=_tile(S, 512), sc=_tile(S, 512))
        x1, h2 = _mix(xp, conv_o.reshape(B * S, c_conv), attn_o.reshape(B * S, c_q),
                      w_out_b, g_post, g_ffn_pre, tm)
        xp = _ffn(x1, h2, wg_b, wu_b, wd_b, g_ffn_post, tm, tf)
        outs[0].append(k.reshape(B, S, N_KV_HEADS, HEAD_DIM))
        outs[1].append(v.reshape(B, S, N_KV_HEADS, HEAD_DIM))
        outs[2].append(ki.reshape(B, S, IDX_DIM))
        outs[3].append(r3(u)[:, S - (cw.shape[0] - 1):, :])

        tms = _tile(Bd * Td, 256)
        u, q, k, v, kb, vb, qi, kw = _proj(xs, g_pre, w_in_b, sizes, tms)
        r3 = lambda a: a.reshape(Bd, Td, a.shape[-1])
        conv_o, conv_st = _conv_sample(r3(u), state_conv[l], cw, cb, lg, lb)
        ki = kw[:, :IDX_DIM]
        pad_rows = lambda a: jnp.pad(r3(a), ((0, 0), (0, LANES - Td), (0, 0)))
        ki_t = jnp.pad(jnp.swapaxes(r3(ki), 1, 2), ((0, 0), (0, 0), (0, LANES - Td))).astype(BF16)
        attn_o = _attn_sample(
            page_table, r3(qi), r3(kw), r3(q), jnp.concatenate([ki_t, ki_t], axis=1),
            pad_rows(kb), pad_rows(vb),
            jnp.swapaxes(cache_idx_k[l], 1, 2),

            cache_k[l].reshape(n_pool, page * N_KV_HEADS, HEAD_DIM),
            cache_v[l].reshape(n_pool, page * N_KV_HEADS, HEAD_DIM),
            n_pg=_tile(page_table.shape[1], 16))
        x1, h2 = _mix(xs, conv_o.reshape(Bd * Td, c_conv), attn_o.reshape(Bd * Td, c_q),
                      w_out_b, g_post, g_ffn_pre, tms)
        xs = _ffn(x1, h2, wg_b, wu_b, wd_b, g_ffn_post, tms, tf)
        outs[4].append(k.reshape(Bd, Td, N_KV_HEADS, HEAD_DIM))
        outs[5].append(v.reshape(Bd, Td, N_KV_HEADS, HEAD_DIM))
        outs[6].append(ki.reshape(Bd, Td, IDX_DIM))
        outs[7].append(conv_st)

    return (xp.reshape(B, S, D), xs.reshape(Bd, Td, D), *[jnp.stack(o) for o in outs])
```

```python
import functools

import jax
import jax.numpy as jnp
from jax import lax
from jax.experimental import pallas as pl
from jax.experimental.pallas import tpu as pltpu

N_HEADS = 8
N_KV_HEADS = 2
N_GROUP = N_HEADS // N_KV_HEADS
HEAD_DIM = 128
N_IDX_HEADS = 16
IDX_DIM = 64
TOPK_MAX = 256
EPS = 1e-6
LANES = 128
SUBLANES = 8
VMEM_LIMIT = 56 * 1024 * 1024

INT_MIN = -2 ** 31
NEG_INF_KEY = (0xFF800000 ^ 0x7FFFFFFF) - 2 ** 32
MASK_BIAS = -1e30
COUNT_ROWS = 64
KV_SLOTS = 6
KV_EARLY = 3
IDX_SLOTS = 8
HEADS_PER_UNIT = 4
LOG2_E = 1.4426950408889634

F32 = jnp.float32
BF16 = jnp.bfloat16
I32 = jnp.int32


def _dot(a, b):
    return jnp.dot(a, b, preferred_element_type=F32)


def _dot_nt(a, b):
    return lax.dot_general(a, b, (((1,), (1,)), ((), ())), preferred_element_type=F32)


def _sigmoid(x):
    return 1.0 / (1.0 + jnp.exp(-x))


def _rms(x, g):
    return x * lax.rsqrt(jnp.mean(x * x, axis=-1, keepdims=True) + EPS) * g


def _sort_key(x):
    b = pltpu.bitcast(x, I32)
    return b ^ ((b >> 31) & 0x7FFFFFFF)


def _params(*sem):
    return pltpu.CompilerParams(dimension_semantics=sem, vmem_limit_bytes=VMEM_LIMIT)


def _resident(shape):
    nd = len(shape)
    return pl.BlockSpec(shape, lambda *_: (0,) * nd, pipeline_mode=pl.Buffered(1))


def _proj_kernel(x_ref, g_ref, w_ref, u_ref, q_ref, k_ref, v_ref, kb_ref, vb_ref,
                 qi_ref, kw_ref, *, offs, attn_scale):
    h = _rms(x_ref[...], g_ref[...]).astype(BF16)
    o_ua, o_ug, o_q, o_k, o_v, o_qi, o_kw, o_end = offs
    ua = _dot(h, w_ref[:, o_ua:o_ug])
    ug = _dot(h, w_ref[:, o_ug:o_q])
    u_ref[...] = ua * _sigmoid(ug)
    q_ref[...] = (_dot(h, w_ref[:, o_q:o_k]) * attn_scale).astype(BF16)
    k = _dot(h, w_ref[:, o_k:o_v])
    k_ref[...] = k
    kb_ref[...] = k.astype(BF16)
    v = _dot(h, w_ref[:, o_v:o_qi])
    v_ref[...] = v
    vb_ref[...] = v.astype(BF16)
    qi_ref[...] = _dot(h, w_ref[:, o_qi:o_kw]).astype(BF16)
    kw_ref[...] = _dot(h, w_ref[:, o_kw:o_end])


def _proj(x, g, w, sizes, tm):
    T, D = x.shape
    c_conv, _, c_q, c_kv, _, c_qi, c_ki, c_wi = sizes
    offs = [0]
    for s in sizes[:6]:
        offs.append(offs[-1] + s)
    offs.append(w.shape[1])
    kw_w = offs[-1] - offs[-2]
    row = lambda n: pl.BlockSpec((tm, n), lambda i: (i, 0))
    out_shape = (
        jax.ShapeDtypeStruct((T, c_conv), F32),
        jax.ShapeDtypeStruct((T, c_q), BF16),
        jax.ShapeDtypeStruct((T, c_kv), F32),
        jax.ShapeDtypeStruct((T, c_kv), F32),
        jax.ShapeDtypeStruct((T, c_kv), BF16),
        jax.ShapeDtypeStruct((T, c_kv), BF16),
        jax.ShapeDtypeStruct((T, c_qi), BF16),
        jax.ShapeDtypeStruct((T, kw_w), F32),
    )
    return pl.pallas_call(
        functools.partial(_proj_kernel, offs=tuple(offs), attn_scale=HEAD_DIM ** -0.5 * LOG2_E),
        grid=(T // tm,),
        in_specs=[row(D), _resident((1, D)), _resident(w.shape)],
        out_specs=tuple(row(s.shape[1]) for s in out_shape),
        out_shape=out_shape,
        compiler_params=_params("parallel"),
        name="proj",
    )(x, g, w)


HIST_PAD = 32


def _conv_body(full_ref, sh_ref, y_ref, w_ref, b_ref, g_ref, beta_ref, o_ref, *, width, tb, cw):
    C = full_ref.shape[1]
    base = HIST_PAD - (width - 1)
    n_b = min(SUBLANES, width)
    n_a = [(width - 1 - b) // SUBLANES + 1 for b in range(n_b)]
    for b in range(n_b):
        rows_b = tb + SUBLANES * (n_a[b] - 1)
        sh_ref[b, 0:rows_b, :] = full_ref[base + b:base + b + rows_b, :]
    for c in range(C // cw):
        cs = slice(c * cw, (c + 1) * cw)
        acc = jnp.zeros((tb, cw), F32)
        for b in range(n_b):
            for a in range(n_a[b]):
                j = b + SUBLANES * a
                acc = acc + sh_ref[b, SUBLANES * a:SUBLANES * a + tb, cs] * w_ref[j:j + 1, cs]
        y_ref[:, cs] = acc + b_ref[:, cs]
    y = y_ref[...]
    mu = jnp.mean(y, axis=-1, keepdims=True)
    yc = y - mu
    z = yc * lax.rsqrt(jnp.mean(yc * yc, axis=-1, keepdims=True) + EPS) * g_ref[...] + beta_ref[...]
    o_ref[...] = (z * _sigmoid(z)).astype(o_ref.dtype)


def _conv_scratch(tb, C, width):
    n_b = min(SUBLANES, width)
    return [pltpu.VMEM((HIST_PAD + tb, C), F32),
            pltpu.VMEM((n_b, tb + SUBLANES * ((width - 1) // SUBLANES), C), F32),
            pltpu.VMEM((tb, C), F32)]


def _conv_prompt_kernel(prev_ref, u_ref, w_ref, b_ref, g_ref, beta_ref, o_ref,
                        full_ref, sh_ref, y_ref, *, width, tb, cw):
    @pl.when(pl.program_id(1) == 0)
    def _():
        full_ref[0:HIST_PAD, :] = jnp.zeros((HIST_PAD, full_ref.shape[1]), F32)

    @pl.when(pl.program_id(1) > 0)
    def _():
        full_ref[0:HIST_PAD, :] = prev_ref[...]

    full_ref[HIST_PAD:HIST_PAD + tb, :] = u_ref[...]
    _conv_body(full_ref, sh_ref, y_ref, w_ref, b_ref, g_ref, beta_ref, o_ref, width=width, tb=tb, cw=cw)


def _conv_prompt(u, conv_w, conv_b, ln_g, ln_b, tb):
    B, S, C = u.shape
    width = conv_w.shape[0]
    r = tb // HIST_PAD
    vec = lambda: _resident((1, C))
    return pl.pallas_call(
        functools.partial(_conv_prompt_kernel, width=width, tb=tb, cw=128),
        grid=(B, S // tb),
        in_specs=[
            pl.BlockSpec((None, HIST_PAD, C), lambda b, i: (b, jnp.maximum(i * r - 1, 0), 0)),
            pl.BlockSpec((None, tb, C), lambda b, i: (b, i, 0)),
            _resident(conv_w.shape), vec(), vec(), vec(),
        ],
        out_specs=pl.BlockSpec((None, tb, C), lambda b, i: (b, i, 0)),
        out_shape=jax.ShapeDtypeStruct((B, S, C), BF16),
        scratch_shapes=_conv_scratch(tb, C, width),
        compiler_params=_params("parallel", "arbitrary"),
        name="conv_prompt",
    )(u, u, conv_w, conv_b, ln_g, ln_b)


def _conv_sample_kernel(hist_ref, u_ref, w_ref, b_ref, g_ref, beta_ref, o_ref, st_ref,
                        full_ref, sh_ref, y_ref, *, width, tb, cw):
    full_ref[0:HIST_PAD, :] = hist_ref[...]
    full_ref[HIST_PAD:HIST_PAD + tb, :] = u_ref[...]
    _conv_body(full_ref, sh_ref, y_ref, w_ref, b_ref, g_ref, beta_ref, o_ref, width=width, tb=tb, cw=cw)
    st_ref[...] = full_ref[HIST_PAD + tb - (width - 1):HIST_PAD + tb, :]


def _conv_sample(u, hist, conv_w, conv_b, ln_g, ln_b):
    B, T, C = u.shape
    width = conv_w.shape[0]
    hist_p = jnp.pad(hist, ((0, 0), (HIST_PAD - (width - 1), 0), (0, 0)))
    vec = lambda: _resident((1, C))
    return pl.pallas_call(
        functools.partial(_conv_sample_kernel, width=width, tb=T, cw=256),
        grid=(B,),
        in_specs=[
            pl.BlockSpec((None, HIST_PAD, C), lambda b: (b, 0, 0)),
            pl.BlockSpec((None, T, C), lambda b: (b, 0, 0)),
            _resident(conv_w.shape), vec(), vec(), vec(),
        ],
        out_specs=(pl.BlockSpec((None, T, C), lambda b: (b, 0, 0)),
                   pl.BlockSpec((None, width - 1, C), lambda b: (b, 0, 0))),
        out_shape=(jax.ShapeDtypeStruct((B, T, C), BF16),
                   jax.ShapeDtypeStruct((B, width - 1, C), F32)),
        scratch_shapes=_conv_scratch(T, C, width),
        compiler_params=_params("parallel"),
        name="conv_sample",
    )(hist_p, u, conv_w, conv_b, ln_g, ln_b)


def _count(fold, rows, pred, row_vals, stage_ref):
    grouped = rows > COUNT_ROWS
    if grouped:
        for n, v in enumerate(row_vals):
            stage_ref[n] = jnp.broadcast_to(v, (rows, LANES))

    def group(cnt, keys, pos0, vals):
        hits = [jnp.where(pred(keys[:, j * LANES:(j + 1) * LANES], pos0 + j * LANES, *vals), 1, 0)
                for j in range(keys.shape[1] // LANES)]
        while len(hits) > 1:
            hits = [a + b for a, b in zip(hits[::2], hits[1::2])] + hits[len(hits) & ~1:]
        return cnt + hits[0]

    def add_block(cnt, keys, pos0):
        if not grouped:
            return group(cnt, keys, pos0, [jnp.broadcast_to(v, (rows, LANES)) for v in row_vals])
        return jnp.concatenate(
            [group(cnt[r:r + COUNT_ROWS], keys[r:r + COUNT_ROWS], pos0,
                   [stage_ref[n, r:r + COUNT_ROWS, :] for n in range(len(row_vals))])
             for r in range(0, rows, COUNT_ROWS)], axis=0)

    cnt = fold(add_block, jnp.zeros((rows, LANES), I32))
    return jnp.sum(cnt.astype(F32), axis=-1, keepdims=True).astype(I32)


def _topk_threshold(fold, update, rows, k_top, n_pos_bits, stage_ref=None):
    def bcast(x):
        return jnp.broadcast_to(x, (rows, LANES))

    def count_ge(cand):
        return _count(fold, rows, lambda keys, pos0, c: keys >= c, [cand], stage_ref)

    def bit_step(it, state):
        ub, c_ub = state
        cand_u = ub | jnp.left_shift(jnp.int32(1), 31 - it)
        cnt = count_ge(cand_u ^ INT_MIN)
        keep = cnt >= k_top
        return jnp.where(keep, cand_u, ub), jnp.where(keep, cnt, c_ub)

    start = (jnp.zeros((rows, 1), I32), jnp.full((rows, 1), k_top + 1, I32))
    ub, c_thr = lax.fori_loop(0, 32, bit_step, start)
    thr = ub ^ INT_MIN
    tie = (c_thr > k_top) & (thr > NEG_INF_KEY)

    @pl.when(jnp.sum(jnp.where(tie, 1.0, 0.0)) > 0.0)
    def _():
        need = k_top - count_ge(thr + 1)
        thr_b = bcast(thr)
        lane = lax.broadcasted_iota(I32, (rows, LANES), 1)

        def tied_before(cut):
            return _count(fold, rows, lambda keys, pos0, t, c, ln: (keys == t) & (ln + pos0 < c),
                          [thr, cut, lane], stage_ref)

        def cut_step(it, cut):
            cand = cut | jnp.left_shift(jnp.int32(1), n_pos_bits - 1 - it)
            return jnp.where(tied_before(cand) <= need, cand, cut)

        cut = lax.fori_loop(0, n_pos_bits, cut_step, jnp.zeros((rows, 1), I32))
        cut_b = bcast(jnp.where(tie, cut, 2 ** n_pos_bits))

        def drop(keys, pos0):
            parts = []
            for j in range(keys.shape[1] // LANES):
                blk = keys[:, j * LANES:(j + 1) * LANES]
                surplus = (blk == thr_b) & (lane + (pos0 + j * LANES) >= cut_b)
                parts.append(jnp.where(surplus, NEG_INF_KEY, blk))
            return jnp.concatenate(parts, axis=-1)

        update(drop)

    return jnp.maximum(thr, NEG_INF_KEY + 1)


def _softmax_step(s, m_ref, l_ref, acc_ref, v, c):
    m_old = m_ref[c]
    m_new = jnp.maximum(m_old, jnp.max(s, axis=-1, keepdims=True))
    alpha = jnp.exp2(m_old - m_new)
    p = jnp.exp2(s - jnp.tile(m_new, (1, s.shape[1] // LANES)))
    l_ref[c] = alpha * l_ref[c] + jnp.sum(p, axis=-1, keepdims=True)
    acc_ref[c] = alpha * acc_ref[c] + _dot(p.astype(BF16), v)
    m_ref[c] = m_new


def _attn_prompt_kernel(qi_ref, kw_ref, q_ref, ki_ref, k_ref, v_ref, o_ref,
                        keys_ref, qh_ref, wb_ref, stage_ref, m_ref, l_ref, acc_ref,
                        *, qb, sc, k_top, index_scale):
    i = pl.program_id(1)
    n_pos_bits = (keys_ref.shape[0] * sc).bit_length()
    n_chunks = ((i + 1) * qb + sc - 1) // sc

    lane = lax.broadcasted_iota(I32, (qb, LANES), 1)
    for h in range(N_IDX_HEADS):
        blk = qi_ref[:, (h // 2) * LANES:(h // 2 + 1) * LANES]
        keep = (lane < IDX_DIM) if h % 2 == 0 else (lane >= IDX_DIM)
        qh_ref[h] = jnp.where(keep, blk, jnp.zeros_like(blk))
        w_h = kw_ref[:, IDX_DIM + h:IDX_DIM + h + 1] * index_scale
        wb_ref[h] = jnp.broadcast_to(w_h, (qb, LANES))

    q_pos = i * qb + lax.broadcasted_iota(I32, (qb, sc), 0)

    def score_chunk(c, carry):
        ks = ki_ref[pl.ds(pl.multiple_of(c * sc, sc), sc), :]
        parts = [jnp.zeros((qb, LANES), F32) for _ in range(sc // LANES)]
        for h in range(N_IDX_HEADS):
            logit = _dot_nt(qh_ref[h], ks)
            w_b = wb_ref[h]
            for j in range(sc // LANES):
                parts[j] = parts[j] + jnp.maximum(logit[:, j * LANES:(j + 1) * LANES], 0.0) * w_b
        score = jnp.concatenate(parts, axis=-1)
        s_pos = c * sc + lax.broadcasted_iota(I32, (qb, sc), 1)
        keys_ref[c] = jnp.where(s_pos <= q_pos, _sort_key(score), NEG_INF_KEY)
        return carry

    lax.fori_loop(0, n_chunks, score_chunk, 0)

    def fold(f, init):
        return lax.fori_loop(0, n_chunks, lambda c, carry: f(carry, keys_ref[c], c * sc), init)

    def update(f):
        def body(c, carry):
            keys_ref[c] = f(keys_ref[c], c * sc)
            return carry
        lax.fori_loop(0, n_chunks, body, 0)

    thr = _topk_threshold(fold, update, qb, k_top, n_pos_bits, stage_ref)
    thr_b = jnp.broadcast_to(thr, (qb, LANES))

    units = [(kv, g0) for kv in range(N_KV_HEADS) for g0 in range(0, N_GROUP, HEADS_PER_UNIT)]
    m_ref[...] = jnp.full(m_ref.shape, MASK_BIAS, F32)
    l_ref[...] = jnp.zeros(l_ref.shape, F32)
    acc_ref[...] = jnp.zeros(acc_ref.shape, F32)

    def attn_chunk(c, carry):
        keys = keys_ref[c]
        bias = jnp.concatenate(
            [jnp.where(keys[:, j * LANES:(j + 1) * LANES] >= thr_b, 0.0, MASK_BIAS)
             for j in range(sc // LANES)], axis=-1)
        bias = jnp.concatenate([bias] * HEADS_PER_UNIT, axis=0)
        start = pl.multiple_of(c * sc, sc)
        for u, (kv, g0) in enumerate(units):
            qc = jnp.concatenate(
                [q_ref[:, (kv * N_GROUP + g) * HEAD_DIM:(kv * N_GROUP + g + 1) * HEAD_DIM]
                 for g in range(g0, g0 + HEADS_PER_UNIT)], axis=0)
            kc = k_ref[pl.ds(start, sc), kv * HEAD_DIM:(kv + 1) * HEAD_DIM]
            vc = v_ref[pl.ds(start, sc), kv * HEAD_DIM:(kv + 1) * HEAD_DIM]
            _softmax_step(_dot_nt(qc, kc) + bias, m_ref, l_ref, acc_ref, vc, u)
        return carry

    lax.fori_loop(0, n_chunks, attn_chunk, 0)

    for u, (kv, g0) in enumerate(units):
        o = acc_ref[u] / l_ref[u]
        for j in range(HEADS_PER_UNIT):
            hd = kv * N_GROUP + g0 + j
            o_ref[:, hd * HEAD_DIM:(hd + 1) * HEAD_DIM] = o[j * qb:(j + 1) * qb, :].astype(o_ref.dtype)


def _attn_prompt(qi, kw, q, ki2, kb, vb, qb, sc):
    B, S, _ = q.shape
    k_top = min(TOPK_MAX, S // 4)
    assert sc >= k_top and S % sc == 0 and S % qb == 0 and sc % qb == 0
    qblk = lambda n: pl.BlockSpec((None, qb, n), lambda b, i: (b, i, 0))
    whole = lambda n: pl.BlockSpec((None, S, n), lambda b, i: (b, 0, 0), pipeline_mode=pl.Buffered(1))
    rows = HEADS_PER_UNIT * qb
    n_units = N_HEADS // HEADS_PER_UNIT
    return pl.pallas_call(
        functools.partial(_attn_prompt_kernel, qb=qb, sc=sc, k_top=k_top,
                          index_scale=(N_IDX_HEADS * IDX_DIM) ** -0.5),
        grid=(B, S // qb),
        in_specs=[qblk(qi.shape[2]), qblk(kw.shape[2]), qblk(q.shape[2]),
                  whole(ki2.shape[2]), whole(kb.shape[2]), whole(vb.shape[2])],
        out_specs=qblk(q.shape[2]),
        out_shape=jax.ShapeDtypeStruct(q.shape, BF16),
        scratch_shapes=[
            pltpu.VMEM((S // sc, qb, sc), I32),
            pltpu.VMEM((N_IDX_HEADS, qb, LANES), BF16),
            pltpu.VMEM((N_IDX_HEADS, qb, LANES), F32),
            pltpu.VMEM((3, qb, LANES), I32),
            pltpu.VMEM((n_units, rows, LANES), F32),
            pltpu.VMEM((n_units, rows, LANES), F32),
            pltpu.VMEM((n_units, rows, HEAD_DIM), F32),
        ],
        compiler_params=_params("parallel", "arbitrary"),
        name="attn_prompt",
    )(qi, kw, q, ki2, kb, vb)


def _attn_sample_kernel(pt_ref, qi_ref, kw_ref, q_ref, kin_ref, kn_ref, vn_ref,
                        ci_hbm, ck_hbm, cv_hbm, o_ref,
                        keys_ref, keyn_ref, ibuf, kbuf, vbuf, m_ref, l_ref, acc_ref,
                        ipg, kpg, vpg, sem_i, sem_k, sem_v,
                        *, n_pg, n_steps, t_new, page, k_top, index_scale):
    b = pl.program_id(0)
    sc = n_pg * page

    def group_copies(src, dst, sem, step, slot):
        return [pltpu.make_async_copy(src.at[pt_ref[b, step * n_pg + p]], dst.at[slot, p], sem.at[slot])
                for p in range(n_pg)]

    def start_group(src, dst, sem, step, slot):
        for cp in group_copies(src, dst, sem, step, slot):
            cp.start()

    def wait_group(src, dst, sem, step, slot):
        for cp in group_copies(src, dst, sem, step, slot):
            cp.wait()

    idx_slots = ipg.shape[0]
    for s0 in range(min(idx_slots, n_steps)):
        start_group(ci_hbm, ipg, sem_i, s0, s0)
    kv_slots = kpg.shape[0]
    kv_ahead = min(kv_slots - 1, n_steps)
    kv_early = min(KV_EARLY, kv_ahead)
    for s0 in range(kv_early):
        start_group(ck_hbm, kpg, sem_k, s0, s0)
        start_group(cv_hbm, vpg, sem_v, s0, s0)

    def head_sum(x):
        return jnp.sum(x.reshape(N_IDX_HEADS, t_new, x.shape[-1]), axis=0)

    qi = qi_ref[...].astype(F32)
    lane = lax.broadcasted_iota(I32, (t_new, LANES), 1)
    pieces, wcols = [], []
    for h in range(N_IDX_HEADS):
        blk = qi[:, (h // 2) * LANES:(h // 2 + 1) * LANES]
        keep = (lane < IDX_DIM) if h % 2 == 0 else (lane >= IDX_DIM)
        pieces.append(jnp.where(keep, blk, 0.0))
        w_h = kw_ref[:, IDX_DIM + h:IDX_DIM + h + 1] * index_scale
        wcols.append(jnp.broadcast_to(w_h, (t_new, LANES)))
    lhs = jnp.concatenate(pieces, axis=0).astype(BF16)
    w_b = jnp.concatenate(wcols, axis=0)

    def scores(key_mat):
        logit = _dot(lhs, key_mat)
        parts = [head_sum(jnp.maximum(logit[:, c * LANES:(c + 1) * LANES], 0.0) * w_b)
                 for c in range(key_mat.shape[1] // LANES)]
        return jnp.concatenate(parts, axis=-1)

    def score_step(s, carry):
        slot = s % idx_slots
        wait_group(ci_hbm, ipg, sem_i, s, slot)
        for p in range(n_pg):
            pg = ipg[slot, p].astype(BF16)
            ibuf[:, p * page:(p + 1) * page] = jnp.concatenate([pg, pg], axis=0)

        @pl.when(s + idx_slots < n_steps)
        def _():
            start_group(ci_hbm, ipg, sem_i, s + idx_slots, slot)

        keys_ref[s] = _sort_key(scores(ibuf[...]))
        return carry

    lax.fori_loop(0, n_steps, score_step, 0)

    for s0 in range(kv_early, kv_ahead):
        start_group(ck_hbm, kpg, sem_k, s0, s0)
        start_group(cv_hbm, vpg, sem_v, s0, s0)

    s_new = scores(kin_ref[...])
    t_pos = lax.broadcasted_iota(I32, (t_new, LANES), 0)
    s_pos = lax.broadcasted_iota(I32, (t_new, LANES), 1)
    keyn_ref[...] = jnp.where(s_pos <= t_pos, _sort_key(s_new), NEG_INF_KEY)

    past = n_steps * sc

    def fold(f, init):
        carry = init
        for c in range(n_steps):
            carry = f(carry, keys_ref[c], c * sc)
        return f(carry, keyn_ref[...], past)

    def update(f):
        for c in range(n_steps):
            keys_ref[c] = f(keys_ref[c], c * sc)
        keyn_ref[...] = f(keyn_ref[...], past)

    thr = _topk_threshold(fold, update, t_new, k_top, (past + LANES).bit_length())
    thr_b = jnp.broadcast_to(thr, (t_new, LANES))
    m_ref[...] = jnp.full(m_ref.shape, MASK_BIAS, F32)
    l_ref[...] = jnp.zeros(l_ref.shape, F32)
    acc_ref[...] = jnp.zeros(acc_ref.shape, F32)

    def attend(keys, k_of, v_of):
        bias = jnp.concatenate(
            [jnp.where(keys[:, c * LANES:(c + 1) * LANES] >= thr_b, 0.0, MASK_BIAS)
             for c in range(keys.shape[1] // LANES)], axis=-1)
        bias = jnp.concatenate([bias] * N_GROUP, axis=0)
        q = q_ref[...].astype(F32)
        for kv in range(N_KV_HEADS):
            qc = jnp.concatenate(
                [q[:, (kv * N_GROUP + g) * HEAD_DIM:(kv * N_GROUP + g + 1) * HEAD_DIM]
                 for g in range(N_GROUP)], axis=0).astype(BF16)
            _softmax_step(_dot_nt(qc, k_of(kv)) + bias, m_ref, l_ref, acc_ref, v_of(kv), kv)

    def attend_step(s, carry):
        slot = s % kv_slots
        ahead = s + kv_slots - 1

        @pl.when(ahead < n_steps)
        def _():
            start_group(ck_hbm, kpg, sem_k, ahead, ahead % kv_slots)
            start_group(cv_hbm, vpg, sem_v, ahead, ahead % kv_slots)

        wait_group(ck_hbm, kpg, sem_k, s, slot)
        wait_group(cv_hbm, vpg, sem_v, s, slot)
        for p in range(n_pg):
            for kv in range(N_KV_HEADS):
                rows_kv = pl.ds(kv, page, stride=N_KV_HEADS)
                kbuf[kv, p * page:(p + 1) * page, :] = kpg[slot, p, rows_kv, :].astype(BF16)
                vbuf[kv, p * page:(p + 1) * page, :] = vpg[slot, p, rows_kv, :].astype(BF16)
        attend(keys_ref[s], lambda kv: kbuf[kv], lambda kv: vbuf[kv])
        return carry

    lax.fori_loop(0, n_steps, attend_step, 0)

    attend(keyn_ref[...],
           lambda kv: kn_ref[:, kv * HEAD_DIM:(kv + 1) * HEAD_DIM],
           lambda kv: vn_ref[:, kv * HEAD_DIM:(kv + 1) * HEAD_DIM])
    for kv in range(N_KV_HEADS):
        o = acc_ref[kv] / l_ref[kv]
        for g in range(N_GROUP):
            hd = kv * N_GROUP + g
            o_ref[:, hd * HEAD_DIM:(hd + 1) * HEAD_DIM] = (
                o[g * t_new:(g + 1) * t_new, :].astype(o_ref.dtype))


def _attn_sample(page_table, qi, kw, q, ki2n_t, kbn, vbn, cache_ik_t, cache_k, cache_v, n_pg):
    B, T, _ = q.shape
    n_pages = page_table.shape[1]
    page = cache_ik_t.shape[2]
    assert n_pages % n_pg == 0 and page == LANES and cache_k.shape[1] == page * N_KV_HEADS
    assert cache_ik_t.shape[1] == IDX_DIM
    n_steps = n_pages // n_pg
    idx_slots = min(IDX_SLOTS, n_steps)
    sc = n_pg * page
    k_top = min(TOPK_MAX, (n_pages * page + T) // 4)
    rows = N_GROUP * T

    per_b = lambda a: pl.BlockSpec((None,) + a.shape[1:], lambda b, pt: (b, 0, 0))
    hbm = pl.BlockSpec(memory_space=pl.ANY)
    grid_spec = pltpu.PrefetchScalarGridSpec(
        num_scalar_prefetch=1,
        grid=(B,),
        in_specs=[per_b(qi), per_b(kw), per_b(q), per_b(ki2n_t), per_b(kbn), per_b(vbn), hbm, hbm, hbm],
        out_specs=pl.BlockSpec((None, T, q.shape[2]), lambda b, pt: (b, 0, 0)),
        scratch_shapes=[
            pltpu.VMEM((n_steps, T, sc), I32),
            pltpu.VMEM((T, LANES), I32),
            pltpu.VMEM((2 * IDX_DIM, sc), BF16),
            pltpu.VMEM((N_KV_HEADS, sc, HEAD_DIM), BF16),
            pltpu.VMEM((N_KV_HEADS, sc, HEAD_DIM), BF16),
            pltpu.VMEM((N_KV_HEADS, rows, LANES), F32),
            pltpu.VMEM((N_KV_HEADS, rows, LANES), F32),
            pltpu.VMEM((N_KV_HEADS, rows, HEAD_DIM), F32),
            pltpu.VMEM((idx_slots, n_pg) + cache_ik_t.shape[1:], cache_ik_t.dtype),
            pltpu.VMEM((KV_SLOTS, n_pg) + cache_k.shape[1:], cache_k.dtype),
            pltpu.VMEM((KV_SLOTS, n_pg) + cache_v.shape[1:], cache_v.dtype),
            pltpu.SemaphoreType.DMA((idx_slots,)),
            pltpu.SemaphoreType.DMA((KV_SLOTS,)),
            pltpu.SemaphoreType.DMA((KV_SLOTS,)),
        ],
    )
    return pl.pallas_call(
        functools.partial(_attn_sample_kernel, n_pg=n_pg, n_steps=n_steps, t_new=T, page=page,
                          k_top=k_top, index_scale=(N_IDX_HEADS * IDX_DIM) ** -0.5),
        grid_spec=grid_spec,
        out_shape=jax.ShapeDtypeStruct(q.shape, BF16),
        compiler_params=_params("arbitrary"),
        name="attn_sample",
    )(page_table, qi, kw, q, ki2n_t, kbn, vbn, cache_ik_t, cache_k, cache_v)


def _mix_kernel(x_ref, c_ref, a_ref, w_ref, g1_ref, g2_ref, x1_ref, h_ref, *, c_conv):
    tm = x_ref.shape[0]
    for r0 in range(0, tm, tm // 2):
        rs = slice(r0, r0 + tm // 2)
        mixed = _dot(c_ref[rs, :], w_ref[0:c_conv, :]) + _dot(a_ref[rs, :], w_ref[c_conv:, :])
        x1 = x_ref[rs, :] + _rms(mixed, g1_ref[...])
        x1_ref[rs, :] = x1
        h_ref[rs, :] = _rms(x1, g2_ref[...]).astype(BF16)


def _mix(x, conv_o, attn_o, w_out, g_post, g_pre, tm):
    T, D = x.shape
    row = lambda n: pl.BlockSpec((tm, n), lambda i: (i, 0))
    return pl.pallas_call(
        functools.partial(_mix_kernel, c_conv=conv_o.shape[1]),
        grid=(T // tm,),
        in_specs=[row(D), row(conv_o.shape[1]), row(attn_o.shape[1]),
                  _resident(w_out.shape), _resident((1, D)), _resident((1, D))],
        out_specs=(row(D), row(D)),
        out_shape=(jax.ShapeDtypeStruct((T, D), F32), jax.ShapeDtypeStruct((T, D), BF16)),
        compiler_params=_params("parallel"),
        name="mix",
    )(x, conv_o, attn_o, w_out, g_post, g_pre)


def _ffn_kernel(x_ref, h_ref, wg_ref, wu_ref, wd_ref, g_ref, y_ref, acc_ref):
    j = pl.program_id(1)

    @pl.when(j == 0)
    def _():
        acc_ref[...] = jnp.zeros(acc_ref.shape, F32)

    h = h_ref[...]
    gate = _dot(h, wg_ref[...])
    a = (gate * _sigmoid(gate) * _dot(h, wu_ref[...])).astype(BF16)
    acc_ref[...] += _dot(a, wd_ref[...])

    @pl.when(j == pl.num_programs(1) - 1)
    def _():
        y_ref[...] = x_ref[...] + _rms(acc_ref[...], g_ref[...])


def _ffn(x1, h, w_gate, w_up, w_down, g_post, tm, tf):
    T, D = x1.shape
    FF = w_gate.shape[1]
    assert FF % tf == 0
    row = lambda: pl.BlockSpec((tm, D), lambda i, j: (i, 0))
    return pl.pallas_call(
        _ffn_kernel,
        grid=(T // tm, FF // tf),
        in_specs=[row(), row(),
                  pl.BlockSpec((D, tf), lambda i, j: (0, j)),
                  pl.BlockSpec((D, tf), lambda i, j: (0, j)),
                  pl.BlockSpec((tf, D), lambda i, j: (j, 0)),
                  _resident((1, D))],
        out_specs=row(),
        out_shape=jax.ShapeDtypeStruct((T, D), F32),
        scratch_shapes=[pltpu.VMEM((tm, D), F32)],
        compiler_params=_params("parallel", "arbitrary"),
        name="ffn",
    )(x1, h, w_gate, w_up, w_down, g_post)


def _tile(n, pref):
    t = min(pref, n)
    while n % t:
        t //= 2
    return t


def _ffn_tile(ff):
    for t in (512, 256, 128):
        if ff % t == 0:
            return t
    return ff


def _dup_lanes(ki):
    return jnp.concatenate([ki, ki], axis=-1).astype(BF16)


def kernel(x_prompt, x_sample, cache_k, cache_v, cache_idx_k, state_conv, page_table,
           g_pre_mix, w_in, conv_w, conv_b, conv_ln_g, conv_ln_b, w_out,
           g_post_mix, g_pre_ffn, w_gate, w_up, w_down, g_post_ffn):
    B, S, D = x_prompt.shape
    Bd, Td, _ = x_sample.shape
    depth = w_in.shape[0]
    c_conv = conv_w.shape[2]
    c_q = N_HEADS * HEAD_DIM
    c_kv = N_KV_HEADS * HEAD_DIM
    c_qi = N_IDX_HEADS * IDX_DIM
    sizes = (c_conv, c_conv, c_q, c_kv, c_kv, c_qi, IDX_DIM, N_IDX_HEADS)
    n_in = sum(sizes)
    assert w_in.shape[2] == n_in
    n_pad = -n_in % LANES
    n_pool, page = cache_k.shape[1], cache_k.shape[2]

    xp = x_prompt.reshape(B * S, D)
    xs = x_sample.reshape(Bd * Td, D)
    outs = [[] for _ in range(8)]
    row2 = lambda a: a.reshape(1, -1)

    for l in range(depth):
        w_in_b = jnp.pad(w_in[l], ((0, 0), (0, n_pad))).astype(BF16)
        w_out_b = w_out[l].astype(BF16)
        wg_b, wu_b, wd_b = w_gate[l].astype(BF16), w_up[l].astype(BF16), w_down[l].astype(BF16)
        g_pre, g_post, g_ffn_pre, g_ffn_post = (row2(g_pre_mix[l]), row2(g_post_mix[l]),
                                                row2(g_pre_ffn[l]), row2(g_post_ffn[l]))
        cw, cb, lg, lb = conv_w[l], row2(conv_b[l]), row2(conv_ln_g[l]), row2(conv_ln_b[l])
        tf = _ffn_tile(wg_b.shape[1])

        tm = _tile(B * S, 512)
        u, q, k, v, kb, vb, qi, kw = _proj(xp, g_pre, w_in_b, sizes, tm)
        r3 = lambda a: a.reshape(B, S, a.shape[-1])
        conv_o = _conv_prompt(r3(u), cw, cb, lg, lb, _tile(S, 256))
        ki = kw[:, :IDX_DIM]
        attn_o = _attn_prompt(r3(qi), r3(kw), r3(q), r3(_dup_lanes(ki)), r3(kb), r3(vb),
                              qb=_tile(S, 512), sc=_tile(S, 512))
        x1, h2 = _mix(xp, conv_o.reshape(B * S, c_conv), attn_o.reshape(B * S, c_q),
                      w_out_b, g_post, g_ffn_pre, tm)
        xp = _ffn(x1, h2, wg_b, wu_b, wd_b, g_ffn_post, tm, tf)
        outs[0].append(k.reshape(B, S, N_KV_HEADS, HEAD_DIM))
        outs[1].append(v.reshape(B, S, N_KV_HEADS, HEAD_DIM))
        outs[2].append(ki.reshape(B, S, IDX_DIM))
        outs[3].append(r3(u)[:, S - (cw.shape[0] - 1):, :])

        tms = _tile(Bd * Td, 256)
        u, q, k, v, kb, vb, qi, kw = _proj(xs, g_pre, w_in_b, sizes, tms)
        r3 = lambda a: a.reshape(Bd, Td, a.shape[-1])
        conv_o, conv_st = _conv_sample(r3(u), state_conv[l], cw, cb, lg, lb)
        ki = kw[:, :IDX_DIM]
        pad_rows = lambda a: jnp.pad(r3(a), ((0, 0), (0, LANES - Td), (0, 0)))
        ki_t = jnp.pad(jnp.swapaxes(r3(ki), 1, 2), ((0, 0), (0, 0), (0, LANES - Td))).astype(BF16)
        attn_o = _attn_sample(
            page_table, r3(qi), r3(kw), r3(q), jnp.concatenate([ki_t, ki_t], axis=1),
            pad_rows(kb), pad_rows(vb),
            jnp.swapaxes(cache_idx_k[l], 1, 2),

            cache_k[l].reshape(n_pool, page * N_KV_HEADS, HEAD_DIM),
            cache_v[l].reshape(n_pool, page * N_KV_HEADS, HEAD_DIM),
            n_pg=_tile(page_table.shape[1], 16))
        x1, h2 = _mix(xs, conv_o.reshape(Bd * Td, c_conv), attn_o.reshape(Bd * Td, c_q),
                      w_out_b, g_post, g_ffn_pre, tms)
        xs = _ffn(x1, h2, wg_b, wu_b, wd_b, g_ffn_post, tms, tf)
        outs[4].append(k.reshape(Bd, Td, N_KV_HEADS, HEAD_DIM))
        outs[5].append(v.reshape(Bd, Td, N_KV_HEADS, HEAD_DIM))
        outs[6].append(ki.reshape(Bd, Td, IDX_DIM))
        outs[7].append(conv_st)

    return (xp.reshape(B, S, D), xs.reshape(Bd, Td, D), *[jnp.stack(o) for o in outs])
```
